```python
import math
import jax, jax.numpy as jnp
from jax import lax
import numpy as np

D_MODEL = 1024
BATCH = 2
SEQ = 8192
DEPTH = 1
DEC_BATCH = 8
DEC_SEQ = 2048
PAST_LEN = 128

GRID_W = 64
PLE_DIM = 256
D_FF = 4 * D_MODEL
NORM_EPS = 1e-6
DN_QK_HEADS = 4
DN_V_HEADS = 8
DN_DK = 128
DN_DV = 128
DN_CHUNK = 64
DN_CONV = 5
DN_QKV_W = 2 * DN_QK_HEADS * DN_DK + DN_V_HEADS * DN_DV
DN_Z_W = DN_V_HEADS * DN_DV
ATT_HEADS = 8
ATT_KV_HEADS = 2
ATT_GROUP = ATT_HEADS // ATT_KV_HEADS
HEAD_DIM = 128
Q_BLOCK = 128
ROPE_THETA = 10000.0
IN_SPLITS = (DN_QKV_W, DN_Z_W, 2 * DN_V_HEADS, 2 * DN_V_HEADS,
             ATT_HEADS * HEAD_DIM, ATT_KV_HEADS * HEAD_DIM, ATT_KV_HEADS * HEAD_DIM,
             D_MODEL, D_MODEL)
IN_COLS = sum(IN_SPLITS)

kernel_name = "hybrid_gdn_axial_gqa_encoder"


def rms_norm(x, gain):
    xf = x.astype(jnp.float32)
    y = xf * lax.rsqrt(jnp.mean(xf * xf, axis=-1, keepdims=True) + NORM_EPS)
    return (y * gain.astype(jnp.float32)).astype(x.dtype)


def l2_norm(x):
    xf = x.astype(jnp.float32)
    return xf * lax.rsqrt(jnp.sum(xf * xf, axis=-1, keepdims=True) + NORM_EPS)


def split_cols(t, sizes):
    out, start = [], 0
    for s in sizes:
        out.append(t[..., start:start + s])
        start += s
    return out


def centred_depthwise_conv(x, w):
    pad = w.shape[0] // 2
    return lax.conv_general_dilated(
        x, w[:, None, :].astype(x.dtype), window_strides=(1,), padding=[(pad, pad)],
        dimension_numbers=("NWC", "WIO", "NWC"), feature_group_count=x.shape[-1])


def gated_delta_chunked(q, k, v, g, beta):
    bsz, n, h, dk = q.shape
    dv = v.shape[-1]
    nc = n // DN_CHUNK

    def chunks(t):
        return t.reshape(bsz, nc, DN_CHUNK, h, -1).transpose(0, 3, 1, 2, 4)

    q = chunks(q) * (dk ** -0.5)
    k = chunks(k)
    v = chunks(v)
    g = g.reshape(bsz, nc, DN_CHUNK, h).transpose(0, 3, 1, 2)
    beta = beta.reshape(bsz, nc, DN_CHUNK, h).transpose(0, 3, 1, 2)
    gc = jnp.cumsum(g, axis=-1)
    idx = jnp.arange(DN_CHUNK)
    incl = idx[:, None] >= idx[None, :]
    strict = idx[:, None] > idx[None, :]
    diff = gc[..., :, None] - gc[..., None, :]
    decay = jnp.where(incl, jnp.exp(jnp.where(incl, diff, 0.0)), 0.0)
    kb = k * beta[..., None]
    a = jnp.where(strict, jnp.einsum("bhncd,bhnsd->bhncs", kb, k) * decay, 0.0)
    rhs = jnp.concatenate([v * beta[..., None], kb * jnp.exp(gc)[..., None]], axis=-1)
    sol = lax.linalg.triangular_solve(a + jnp.eye(DN_CHUNK, dtype=a.dtype), rhs,
                                      left_side=True, lower=True, unit_diagonal=True)
    u, w = sol[..., :dv], sol[..., dv:]
    aqk = jnp.einsum("bhncd,bhnsd->bhncs", q, k) * decay
    gl = gc[..., -1]
    kd = k * jnp.exp(gl[..., None] - gc)[..., None]
    qd = q * jnp.exp(gc)[..., None]
    xs = tuple(jnp.moveaxis(t, 2, 0) for t in (qd, aqk, u, w, kd, gl))

    def step(state, inp):
        qd_i, aqk_i, u_i, w_i, kd_i, gl_i = inp
        v_new = u_i - jnp.einsum("bhck,bhkv->bhcv", w_i, state)
        o = jnp.einsum("bhck,bhkv->bhcv", qd_i, state) + jnp.einsum("bhcs,bhsv->bhcv", aqk_i, v_new)
        state = state * jnp.exp(gl_i)[..., None, None] + jnp.einsum("bhck,bhcv->bhkv", kd_i, v_new)
        return state, o

    s0 = jnp.zeros((bsz, h, dk, dv), jnp.float32)
    _, o = lax.scan(step, s0, xs)
    return o.transpose(1, 0, 3, 2, 4).reshape(bsz, n, h, dv)


def axial_rope_angles(n):
    rows = n // GRID_W
    r = jnp.repeat(jnp.arange(rows), GRID_W).astype(jnp.float32)
    c = jnp.tile(jnp.arange(GRID_W), rows).astype(jnp.float32)
    half = HEAD_DIM // 2
    inv = jnp.power(jnp.float32(ROPE_THETA), -jnp.arange(0, half, 2, dtype=jnp.float32) / half)
    return jnp.stack([r[:, None] * inv, c[:, None] * inv], axis=1)


def apply_axial_rope(x, ang):
    bsz, n, h, d = x.shape
    xr = x.astype(jnp.float32).reshape(bsz, n, h, 2, 2, d // 4)
    cos = jnp.cos(ang)[:, None]
    sin = jnp.sin(ang)[:, None]
    x1, x2 = xr[..., 0, :], xr[..., 1, :]
    out = jnp.stack([x1 * cos - x2 * sin, x2 * cos + x1 * sin], axis=-2)
    return out.reshape(bsz, n, h, d).astype(x.dtype)


def block_attention(q, k, v):
    bsz, n, _, d = q.shape
    nb = n // Q_BLOCK
    qb = q.reshape(bsz, nb, Q_BLOCK, ATT_KV_HEADS, ATT_GROUP, d).transpose(1, 0, 2, 3, 4, 5)
    scale = d ** -0.5

    def one_block(qblk):
        s = jnp.einsum("bqhgd,bkhd->bhgqk", qblk, k).astype(jnp.float32) * scale
        p = jax.nn.softmax(s, axis=-1)
        return jnp.einsum("bhgqk,bkhd->bqhgd", p.astype(v.dtype), v)

    o = lax.map(one_block, qb)
    return o.transpose(1, 0, 2, 3, 4, 5).reshape(bsz, n, ATT_HEADS * d)


def encoder_layer(h, pe, norm_mix, w_in, dn_conv_w, dn_a_log, dn_dt_bias, dn_out_norm,
                  w_dn_branch, q_norm, k_norm, w_attn_branch, w_o, norm_mlp, w_up, w_down,
                  norm_ple, w_ple_gate, w_ple):
    bsz, n, _ = h.shape
    u = rms_norm(h, norm_mix)
    proj = u @ w_in
    dn_qkv, dn_z, dn_b, dn_a, at_q, at_k, at_v, gate_dn, gate_at = split_cols(proj, IN_SPLITS)

    qkv = jax.nn.silu(centred_depthwise_conv(dn_qkv, dn_conv_w))
    dq, dk_, dv_ = split_cols(qkv, (DN_QK_HEADS * DN_DK, DN_QK_HEADS * DN_DK, DN_V_HEADS * DN_DV))
    rep = DN_V_HEADS // DN_QK_HEADS
    dq = jnp.repeat(l2_norm(dq.reshape(bsz, n, DN_QK_HEADS, DN_DK)), rep, axis=2)
    dk_ = jnp.repeat(l2_norm(dk_.reshape(bsz, n, DN_QK_HEADS, DN_DK)), rep, axis=2)
    dv_ = dv_.reshape(bsz, n, DN_V_HEADS, DN_DV).astype(jnp.float32)
    beta = jax.nn.sigmoid(dn_b.astype(jnp.float32).reshape(bsz, n, 2, DN_V_HEADS))
    g = -jnp.exp(dn_a_log.astype(jnp.float32)) * jax.nn.softplus(
        dn_a.astype(jnp.float32).reshape(bsz, n, 2, DN_V_HEADS) + dn_dt_bias.astype(jnp.float32))
    o_fwd = gated_delta_chunked(dq, dk_, dv_, g[:, :, 0], beta[:, :, 0])
    flip = lambda t: jnp.flip(t, axis=1)
    o_bwd = flip(gated_delta_chunked(flip(dq), flip(dk_), flip(dv_), flip(g[:, :, 1]), flip(beta[:, :, 1])))
    o_dn = rms_norm(o_fwd + o_bwd, dn_out_norm) * jax.nn.silu(
        dn_z.reshape(bsz, n, DN_V_HEADS, DN_DV).astype(jnp.float32))
    y_dn = o_dn.reshape(bsz, n, DN_Z_W).astype(h.dtype) @ w_dn_branch

    ang = axial_rope_angles(n)
    aq = apply_axial_rope(rms_norm(at_q.reshape(bsz, n, ATT_HEADS, HEAD_DIM), q_norm), ang)
    ak = apply_axial_rope(rms_norm(at_k.reshape(bsz, n, ATT_KV_HEADS, HEAD_DIM), k_norm), ang)
    av = at_v.reshape(bsz, n, ATT_KV_HEADS, HEAD_DIM)
    y_at = block_attention(aq, ak, av) @ w_attn_branch

    merged = jax.nn.sigmoid(gate_dn) * y_dn + jax.nn.sigmoid(gate_at) * y_at
    h = h + merged @ w_o

    hm = rms_norm(h, norm_mlp) @ w_up
    h = h + jnp.square(jax.nn.relu(hm)) @ w_down

    gate = jax.nn.sigmoid(rms_norm(h, norm_ple) @ w_ple_gate)
    h = h + gate * (pe.astype(h.dtype) @ w_ple)
    return h


def encoder_trunk(x, p, weights):
    h = x
    for i in range(DEPTH):
        h = encoder_layer(h, p[i], *[w[i] for w in weights])
    return h


def setup_inputs(seed: int = 0) -> dict:
    key = jax.random.key(seed)
    ks = jax.random.split(key, 22)
    f32 = jnp.float32

    def nrm(k, shape, scale):
        return jax.random.normal(k, shape, f32) * scale

    def gain(k, shape):
        return 1.0 + 0.02 * jax.random.normal(k, shape, f32)

    dt = jnp.exp(jax.random.uniform(ks[8], (DEPTH, 2, DN_V_HEADS), f32, math.log(1e-3), math.log(1e-1)))
    return {
        "x_prompt": nrm(ks[0], (BATCH, SEQ, D_MODEL), 1.0),
        "x_sample": nrm(ks[1], (DEC_BATCH, DEC_SEQ, D_MODEL), 1.0),
        "p_prompt": nrm(ks[2], (DEPTH, BATCH, SEQ, PLE_DIM), 1.0),
        "p_sample": nrm(ks[3], (DEPTH, DEC_BATCH, DEC_SEQ, PLE_DIM), 1.0),
        "norm_mix": gain(ks[4], (DEPTH, D_MODEL)),
        "w_in": nrm(ks[5], (DEPTH, D_MODEL, IN_COLS), D_MODEL ** -0.5),
        "dn_conv_w": nrm(ks[6], (DEPTH, DN_CONV, DN_QKV_W), DN_CONV ** -0.5),
        "dn_a_log": jnp.log(jax.random.uniform(ks[7], (DEPTH, 2, DN_V_HEADS), f32, 1.0, 16.0)),
        "dn_dt_bias": jnp.log(jnp.expm1(dt)),
        "dn_out_norm": gain(ks[9], (DEPTH, DN_DV)),
        "w_dn_branch": nrm(ks[10], (DEPTH, DN_Z_W, D_MODEL), DN_Z_W ** -0.5),
        "q_norm": gain(ks[11], (DEPTH, HEAD_DIM)),
        "k_norm": gain(ks[12], (DEPTH, HEAD_DIM)),
        "w_attn_branch": nrm(ks[13], (DEPTH, ATT_HEADS * HEAD_DIM, D_MODEL), (ATT_HEADS * HEAD_DIM) ** -0.5),
        "w_o": nrm(ks[14], (DEPTH, D_MODEL, D_MODEL), D_MODEL ** -0.5),
        "norm_mlp": gain(ks[15], (DEPTH, D_MODEL)),
        "w_up": nrm(ks[16], (DEPTH, D_MODEL, D_FF), D_MODEL ** -0.5),
        "w_down": nrm(ks[17], (DEPTH, D_FF, D_MODEL), D_FF ** -0.5),
        "norm_ple": gain(ks[18], (DEPTH, D_MODEL)),
        "w_ple_gate": nrm(ks[19], (DEPTH, D_MODEL, D_MODEL), D_MODEL ** -0.5),
        "w_ple": nrm(ks[20], (DEPTH, PLE_DIM, D_MODEL), PLE_DIM ** -0.5),
    }


def reference(x_prompt, x_sample, p_prompt, p_sample, norm_mix, w_in, dn_conv_w, dn_a_log,
              dn_dt_bias, dn_out_norm, w_dn_branch, q_norm, k_norm, w_attn_branch, w_o,
              norm_mlp, w_up, w_down, norm_ple, w_ple_gate, w_ple):
    weights = (norm_mix, w_in, dn_conv_w, dn_a_log, dn_dt_bias, dn_out_norm, w_dn_branch,
               q_norm, k_norm, w_attn_branch, w_o, norm_mlp, w_up, w_down, norm_ple,
               w_ple_gate, w_ple)
    y_prompt = encoder_trunk(x_prompt, p_prompt, weights)
    y_sample = encoder_trunk(x_sample, p_sample, weights)
    return (y_prompt, y_sample)
```

```python
import functools
import math

import jax
import jax.numpy as jnp
from jax import lax
from jax.experimental import pallas as pl
from jax.experimental.pallas import tpu as pltpu

F32 = jnp.float32
BF16 = jnp.bfloat16

D_MODEL = 1024
PLE_DIM = 256
D_FF = 4 * D_MODEL
NORM_EPS = 1e-6
GRID_W = 64
DN_QK_HEADS = 4
DN_V_HEADS = 8
DN_DK = 128
DN_DV = 128
DN_CHUNK = 64
DN_CONV = 5
DN_QK_W = DN_QK_HEADS * DN_DK
DN_V_W = DN_V_HEADS * DN_DV
ATT_HEADS = 8
ATT_KV_HEADS = 2
ATT_GROUP = ATT_HEADS // ATT_KV_HEADS
HEAD_DIM = 128
ROPE_THETA = 10000.0
ATT_Q_W = ATT_HEADS * HEAD_DIM
ATT_KV_W = ATT_KV_HEADS * HEAD_DIM

LANES = 128
SUBLANES = 8
VMEM_LIMIT_BYTES = 56 * 1024 * 1024

DN_GROUP = 4
DN_GM = DN_GROUP * DN_CHUNK
AUX_W = LANES
AUX_GC, AUX_BETA, AUX_EGC, AUX_EGL, AUX_EG = 0, 1, 2, 3, 4


def _cparams(semantics):
    return pltpu.CompilerParams(dimension_semantics=semantics, vmem_limit_bytes=VMEM_LIMIT_BYTES)


def _resident(shape):
    nd = len(shape)
    return pl.BlockSpec(shape, lambda *_: (0,) * nd, pipeline_mode=pl.Buffered(1))


def _rms(x, gain):
    ms = jnp.mean(x * x, axis=-1, keepdims=True)
    return x * lax.rsqrt(ms + NORM_EPS) * gain


def _mm(a, b):
    return jnp.dot(a.astype(BF16), b.astype(BF16), preferred_element_type=F32)


def _mm_nt(a, b):
    return lax.dot_general(a.astype(BF16), b.astype(BF16), (((1,), (1,)), ((), ())),
                           preferred_element_type=F32)


def _mm_tn(a, b):
    return lax.dot_general(a.astype(BF16), b.astype(BF16), (((0,), (0,)), ((), ())),
                           preferred_element_type=F32)


def _sigmoid(x):
    return 1.0 / (1.0 + jnp.exp(-x))


def _silu(x):
    return x * _sigmoid(x)


def _swap32(x):
    lane = lax.broadcasted_iota(jnp.int32, x.shape, 1)
    first = (lane & 32) == 0
    return jnp.where(first, pltpu.roll(x, LANES - 32, 1), pltpu.roll(x, 32, 1))


def _norm_rope(x, gain, cos, sin, scale):
    y = _rms(x, gain)
    return (y * cos + _swap32(y) * sin) * scale


def _in_proj_kernel(x_ref, gain_ref, cos_ref, sin_ref, qn_ref, kn_ref,
                    wdq_ref, wdk_ref, wdv_ref, wz_ref, wba_ref, waq_ref, wak_ref, wav_ref,
                    wgd_ref, wga_ref,
                    dq_o, dk_o, dv_o, z_o, ba_o, aq_o, ak_o, av_o, gd_o, ga_o):
    u = _rms(x_ref[...], gain_ref[...]).astype(BF16)

    def proj(w_ref):
        return jnp.dot(u, w_ref[...], preferred_element_type=F32)

    dq_o[...] = proj(wdq_ref)
    dk_o[...] = proj(wdk_ref)
    dv_o[...] = proj(wdv_ref)
    z_o[...] = proj(wz_ref).astype(z_o.dtype)
    ba_o[...] = proj(wba_ref)
    av_o[...] = proj(wav_ref).astype(av_o.dtype)
    gd_o[...] = proj(wgd_ref).astype(gd_o.dtype)
    ga_o[...] = proj(wga_ref).astype(ga_o.dtype)

    cos = cos_ref[...]
    sin = sin_ref[...]
    q = proj(waq_ref)
    for h in range(ATT_HEADS):
        sl = slice(h * HEAD_DIM, (h + 1) * HEAD_DIM)
        aq_o[:, sl] = _norm_rope(q[:, sl], qn_ref[...], cos, sin, HEAD_DIM ** -0.5).astype(aq_o.dtype)
    k = proj(wak_ref)
    for h in range(ATT_KV_HEADS):
        sl = slice(h * HEAD_DIM, (h + 1) * HEAD_DIM)
        ak_o[:, sl] = _norm_rope(k[:, sl], kn_ref[...], cos, sin, 1.0).astype(ak_o.dtype)


def _in_proj(x, gain, cos, sin, qn, kn, ws, *, seq, tm):
    t = x.shape[0]
    nseq = seq // tm
    row = lambda w: pl.BlockSpec((tm, w), lambda i: (i, 0))
    tab = pl.BlockSpec((tm, HEAD_DIM), lambda i: (i % nseq, 0))
    widths = [DN_QK_W, DN_QK_W, DN_V_W, DN_V_W, 2 * LANES, ATT_Q_W, ATT_KV_W, ATT_KV_W, D_MODEL, D_MODEL]
    dtypes = [F32, F32, F32, BF16, F32, BF16, BF16, BF16, BF16, BF16]
    return pl.pallas_call(
        _in_proj_kernel,
        grid=(t // tm,),
        in_specs=[row(D_MODEL), _resident((1, D_MODEL)), tab, tab,
                  _resident((1, HEAD_DIM)), _resident((1, HEAD_DIM))]
        + [_resident(w.shape) for w in ws],
        out_specs=[row(w) for w in widths],
        out_shape=[jax.ShapeDtypeStruct((t, w), d) for w, d in zip(widths, dtypes)],
        compiler_params=_cparams(("parallel",)),
        name="in_proj",
    )(x, gain, cos, sin, qn, kn, *ws)


HALO = SUBLANES


def _conv_silu(cur_ref, prev_ref, nxt_ref, w_ref, ext_ref, first, last, tm):
    width = cur_ref.shape[1]
    ext_ref[0:HALO, 0:width] = jnp.where(first, 0.0, prev_ref[...])
    ext_ref[HALO:HALO + tm, 0:width] = cur_ref[...]
    ext_ref[HALO + tm:HALO + tm + HALO, 0:width] = jnp.where(last, 0.0, nxt_ref[...])
    pad = DN_CONV // 2
    acc = None
    for tap in range(DN_CONV):
        start = HALO - pad + tap
        term = ext_ref[start:start + tm, 0:width] * w_ref[tap:tap + 1, :]
        acc = term if acc is None else acc + term
    return _silu(acc)


def _l2norm_heads(x, scale):
    outs = []
    for h in range(DN_QK_HEADS):
        xh = x[:, h * DN_DK:(h + 1) * DN_DK]
        r = lax.rsqrt(jnp.sum(xh * xh, axis=-1, keepdims=True) + NORM_EPS)
        outs.append(xh * (r * scale))
    return jnp.concatenate(outs, axis=1)


def _split3(x):
    hi = x.astype(BF16)
    r1 = x - hi.astype(F32)
    mid = r1.astype(BF16)
    lo = (r1 - mid.astype(F32)).astype(BF16)
    return hi, mid, lo


def _ones_mm(mask_bf, x):
    hi, mid, lo = _split3(x)
    dot = lambda p: jnp.dot(mask_bf, p, preferred_element_type=F32)
    return dot(hi) + dot(mid) + dot(lo)


def _place(x, src, dst):
    shift = (dst - src) % LANES
    return x if shift == 0 else pltpu.roll(x, shift, 1)


def _dn_prep_kernel(q_ref, qp_ref, qn_ref, k_ref, kp_ref, kn_ref, v_ref, vp_ref, vn_ref,
                    ba_ref, wq_ref, wk_ref, wv_ref, alog_ref, dtb_ref,
                    q_o, k_o, v_o, aux_o, ext_ref, *, tm, nseq):
    i = pl.program_id(0)
    first = (i % nseq) == 0
    last = (i % nseq) == nseq - 1
    q = _conv_silu(q_ref, qp_ref, qn_ref, wq_ref, ext_ref, first, last, tm)
    q_o[...] = _l2norm_heads(q, DN_DK ** -0.5)
    k = _conv_silu(k_ref, kp_ref, kn_ref, wk_ref, ext_ref, first, last, tm)
    k_o[...] = _l2norm_heads(k, 1.0)
    v_o[...] = _conv_silu(v_ref, vp_ref, vn_ref, wv_ref, ext_ref, first, last, tm)

    ba = ba_ref[...]
    beta = _sigmoid(ba[:, 0:LANES])
    a = ba[:, LANES:2 * LANES] + dtb_ref[...]
    softplus = jnp.maximum(a, 0.0) + jnp.log(1.0 + jnp.exp(-jnp.abs(a)))
    g = -jnp.exp(alog_ref[...]) * softplus

    ri = lax.broadcasted_iota(jnp.int32, (tm, tm), 0)
    ci = lax.broadcasted_iota(jnp.int32, (tm, tm), 1)
    same = lax.shift_right_logical(ri, 6) == lax.shift_right_logical(ci, 6)
    one = lambda m: jnp.where(m, 1.0, 0.0).astype(BF16)
    pre = _ones_mm(one(same & (ci <= ri)), g)
    suf = _ones_mm(one(same & (ci >= ri)), g)
    tot = _ones_mm(one(same), g)
    lane = lax.broadcasted_iota(jnp.int32, (tm, LANES), 1)
    gc = jnp.where(lane < DN_V_HEADS, pre, suf)
    egc = jnp.exp(gc)
    egl = jnp.exp(tot - gc)
    eg = jnp.exp(tot)
    for d in range(2):
        src = d * DN_V_HEADS
        aux = jnp.zeros((tm, AUX_W), F32)
        for field, val in ((AUX_GC, gc), (AUX_BETA, beta), (AUX_EGC, egc), (AUX_EGL, egl), (AUX_EG, eg)):
            dst = field * DN_V_HEADS
            aux = jnp.where((lane >= dst) & (lane < dst + DN_V_HEADS), _place(val, src, dst), aux)
        aux_o[d] = aux


def _dn_prep(dq, dk, dv, ba, wq, wk, wv, alog, dtb, *, seq, tm):
    t = dq.shape[0]
    nseq = seq // tm
    hb = tm // HALO
    nhb = t // HALO

    def trio(w):
        return [pl.BlockSpec((tm, w), lambda i: (i, 0)),
                pl.BlockSpec((HALO, w), lambda i: (jnp.maximum(i * hb - 1, 0), 0)),
                pl.BlockSpec((HALO, w), lambda i: (jnp.minimum((i + 1) * hb, nhb - 1), 0))]

    row = lambda w: pl.BlockSpec((tm, w), lambda i: (i, 0))
    return pl.pallas_call(
        functools.partial(_dn_prep_kernel, tm=tm, nseq=nseq),
        grid=(t // tm,),
        in_specs=trio(DN_QK_W) + trio(DN_QK_W) + trio(DN_V_W) + [row(2 * LANES)]
        + [_resident(wq.shape), _resident(wk.shape), _resident(wv.shape),
           _resident(alog.shape), _resident(dtb.shape)],
        out_specs=[row(DN_QK_W), row(DN_QK_W), row(DN_V_W),
                   pl.BlockSpec((2, tm, AUX_W), lambda i: (0, i, 0))],
        out_shape=[jax.ShapeDtypeStruct((t, DN_QK_W), F32), jax.ShapeDtypeStruct((t, DN_QK_W), F32),
                   jax.ShapeDtypeStruct((t, DN_V_W), F32), jax.ShapeDtypeStruct((2, t, AUX_W), F32)],
        scratch_shapes=[pltpu.VMEM((tm + 2 * HALO, DN_V_W), F32)],
        compiler_params=_cparams(("parallel",)),
        name="dn_prep",
    )(dq, dq, dq, dk, dk, dk, dv, dv, dv, ba, wq, wk, wv, alog, dtb)


def _unit_lower_inverse(a, eye, m16, m32):
    a16 = jnp.where(m16, a, 0.0)
    a32 = jnp.where(m32 & ~m16, a, 0.0)
    a64 = jnp.where(m32, 0.0, a)
    n = -a16
    x = eye + n
    p = _mm(n, n)
    x = x + _mm(x, p)
    p = _mm(p, p)
    x = x + _mm(x, p)
    p = _mm(p, p)
    x = x + _mm(x, p)
    x = x - _mm(x, _mm(a32, x))
    x = x - _mm(x, _mm(a64, x))
    return x


def _delta_rule_kernel(q_ref, k_ref, v_ref, aux_ref, o_ref, s_ref, *, cb):
    d = pl.program_id(1)
    j = pl.program_id(2)

    @pl.when(j == 0)
    def _():
        s_ref[...] = jnp.zeros_like(s_ref)

    gm = DN_GM
    ri = lax.broadcasted_iota(jnp.int32, (gm, gm), 0)
    ci = lax.broadcasted_iota(jnp.int32, (gm, gm), 1)
    blk = lambda x, s: lax.shift_right_logical(x, s)
    same = blk(ri, 6) == blk(ci, 6)
    rel = (ri - ci) * (1 - 2 * d)
    incl = same & (rel >= 0)
    strict = same & (rel > 0)
    m16 = blk(ri, 4) == blk(ci, 4)
    m32 = blk(ri, 5) == blk(ci, 5)
    eye = jnp.where(ri == ci, 1.0, 0.0).astype(F32)

    def chunk(jj, carry):
        c = jnp.where(d == 0, jj, cb - 1 - jj)
        r0 = pl.multiple_of(c * DN_CHUNK, DN_CHUNK)
        rows = pl.ds(r0, DN_CHUNK)
        aux = aux_ref[0, rows, :]
        for hg in range(DN_V_HEADS // DN_GROUP):
            h0 = hg * DN_GROUP

            def col(field):
                base = field * DN_V_HEADS + h0
                return jnp.concatenate([aux[:, base + r:base + r + 1] for r in range(DN_GROUP)], axis=0)

            gc_col, beta_col, egc_col, egl_col = col(AUX_GC), col(AUX_BETA), col(AUX_EGC), col(AUX_EGL)
            gc_row = jnp.transpose(jnp.broadcast_to(gc_col, (gm, LANES)))[0:1, :]

            qk_heads = [(h0 + r) // (DN_V_HEADS // DN_QK_HEADS) for r in range(DN_GROUP)]
            ks = jnp.concatenate([k_ref[rows, h * DN_DK:(h + 1) * DN_DK] for h in qk_heads], axis=0)
            qs = jnp.concatenate([q_ref[rows, h * DN_DK:(h + 1) * DN_DK] for h in qk_heads], axis=0)
            vs = jnp.concatenate([v_ref[rows, (h0 + r) * DN_DV:(h0 + r + 1) * DN_DV]
                                  for r in range(DN_GROUP)], axis=0)

            gram = _mm_nt(ks, ks)
            qk = _mm_nt(qs, ks)
            decay = jnp.where(incl, jnp.exp(jnp.where(incl, gc_col - gc_row, 0.0)), 0.0)
            a = jnp.where(strict, beta_col * gram * decay, 0.0)
            aqk = qk * decay
            tinv = _unit_lower_inverse(a, eye, m16, m32)
            rhs = jnp.concatenate([vs * beta_col, ks * (beta_col * egc_col)], axis=1)
            sol = _mm(tinv, rhs)
            u = sol[:, 0:DN_DV]
            w = sol[:, DN_DV:DN_DV + DN_DK]
            qd = qs * egc_col
            kd = ks * egl_col

            vnew, oq = [], []
            for r in range(DN_GROUP):
                h = h0 + r
                rs = slice(r * DN_CHUNK, (r + 1) * DN_CHUNK)
                state = s_ref[:, h * DN_DV:(h + 1) * DN_DV]
                res = _mm(jnp.concatenate([w[rs], qd[rs]], axis=0), state)
                vn = u[rs] - res[0:DN_CHUNK]
                vnew.append(vn)
                oq.append(res[DN_CHUNK:2 * DN_CHUNK])
                eg = aux[0:1, AUX_EG * DN_V_HEADS + h:AUX_EG * DN_V_HEADS + h + 1]
                s_ref[:, h * DN_DV:(h + 1) * DN_DV] = state * eg + _mm_tn(kd[rs], vn)
            out = jnp.concatenate(oq, axis=0) + _mm(aqk, jnp.concatenate(vnew, axis=0))
            for r in range(DN_GROUP):
                h = h0 + r
                o_ref[0, rows, h * DN_DV:(h + 1) * DN_DV] = out[r * DN_CHUNK:(r + 1) * DN_CHUNK]
        return carry

    lax.fori_loop(0, cb, chunk, 0)


def _delta_rule(q, k, v, aux, *, batch, seq, cb):
    t = q.shape[0]
    rb = cb * DN_CHUNK
    nb = seq // rb

    def rblock(b, d, j):
        return b * nb + jnp.where(d == 0, j, nb - 1 - j)

    return pl.pallas_call(
        functools.partial(_delta_rule_kernel, cb=cb),
        grid=(batch, 2, nb),
        in_specs=[pl.BlockSpec((rb, DN_QK_W), lambda b, d, j: (rblock(b, d, j), 0)),
                  pl.BlockSpec((rb, DN_QK_W), lambda b, d, j: (rblock(b, d, j), 0)),
                  pl.BlockSpec((rb, DN_V_W), lambda b, d, j: (rblock(b, d, j), 0)),
                  pl.BlockSpec((1, rb, AUX_W), lambda b, d, j: (d, rblock(b, d, j), 0))],
        out_specs=pl.BlockSpec((1, rb, DN_V_W), lambda b, d, j: (d, rblock(b, d, j), 0)),
        out_shape=jax.ShapeDtypeStruct((2, t, DN_V_W), F32),
        scratch_shapes=[pltpu.VMEM((DN_DK, DN_V_HEADS * DN_DV), F32)],
        compiler_params=_cparams(("parallel", "parallel", "arbitrary")),
        name="delta_rule",
    )(q, k, v, aux)


def _attention_kernel(q_ref, k_ref, v_ref, o_ref, *, tq, tk, seq):
    rows = ATT_GROUP * tq
    q = jnp.concatenate([q_ref[:, g * HEAD_DIM:(g + 1) * HEAD_DIM] for g in range(ATT_GROUP)], axis=0)

    def step(c, carry):
        m, l, acc = carry
        ks = pl.ds(pl.multiple_of(c * tk, tk), tk)
        s = lax.dot_general(q, k_ref[ks, :], (((1,), (1,)), ((), ())), preferred_element_type=F32)
        m_new = jnp.maximum(m, jnp.max(s, axis=-1, keepdims=True))
        alpha = jnp.exp(m - m_new)
        p = jnp.exp(s - m_new)
        l = alpha * l + jnp.sum(p, axis=-1, keepdims=True)
        acc = alpha * acc + jnp.dot(p.astype(BF16), v_ref[ks, :], preferred_element_type=F32)
        return m_new, l, acc

    init = (jnp.full((rows, 1), -jnp.inf, F32), jnp.zeros((rows, 1), F32), jnp.zeros((rows, HEAD_DIM), F32))
    _, l, acc = lax.fori_loop(0, seq // tk, step, init)
    out = acc / l
    for g in range(ATT_GROUP):
        o_ref[:, g * HEAD_DIM:(g + 1) * HEAD_DIM] = out[g * tq:(g + 1) * tq].astype(o_ref.dtype)


def _attention(aq, ak, av, *, batch, seq, tq, tk):
    t = aq.shape[0]
    nq = seq // tq
    gw = ATT_GROUP * HEAD_DIM
    return pl.pallas_call(
        functools.partial(_attention_kernel, tq=tq, tk=tk, seq=seq),
        grid=(batch, ATT_KV_HEADS, nq),
        in_specs=[pl.BlockSpec((tq, gw), lambda b, g, i: (b * nq + i, g)),
                  pl.BlockSpec((seq, HEAD_DIM), lambda b, g, i: (b, g)),
                  pl.BlockSpec((seq, HEAD_DIM), lambda b, g, i: (b, g))],
        out_specs=pl.BlockSpec((tq, gw), lambda b, g, i: (b * nq + i, g)),
        out_shape=jax.ShapeDtypeStruct((t, ATT_Q_W), BF16),
        compiler_params=_cparams(("parallel", "parallel", "parallel")),
        name="attention",
    )(aq, ak, av)


def _merge_kernel(h_ref, o_ref, z_ref, att_ref, gd_ref, ga_ref, on_ref, wdn_ref, wat_ref, wo_ref, out_ref):
    o = o_ref[0] + o_ref[1]
    z = z_ref[...].astype(F32)
    parts = []
    for h in range(DN_V_HEADS):
        sl = slice(h * DN_DV, (h + 1) * DN_DV)
        parts.append((_rms(o[:, sl], on_ref[...]) * _silu(z[:, sl])).astype(BF16))
    o_dn = jnp.concatenate(parts, axis=1)
    y_dn = jnp.dot(o_dn, wdn_ref[...], preferred_element_type=F32)
    y_at = jnp.dot(att_ref[...], wat_ref[...], preferred_element_type=F32)
    merged = _sigmoid(gd_ref[...].astype(F32)) * y_dn + _sigmoid(ga_ref[...].astype(F32)) * y_at
    out_ref[...] = h_ref[...] + jnp.dot(merged.astype(BF16), wo_ref[...], preferred_element_type=F32)


def _merge(h, o, z, att, gd, ga, on, wdn, wat, wo, *, tm):
    t = h.shape[0]
    row = lambda w: pl.BlockSpec((tm, w), lambda i: (i, 0))
    return pl.pallas_call(
        _merge_kernel,
        grid=(t // tm,),
        in_specs=[row(D_MODEL), pl.BlockSpec((2, tm, DN_V_W), lambda i: (0, i, 0)), row(DN_V_W),
                  row(ATT_Q_W), row(D_MODEL), row(D_MODEL), _resident(on.shape),
                  _resident(wdn.shape), _resident(wat.shape), _resident(wo.shape)],
        out_specs=row(D_MODEL),
        out_shape=jax.ShapeDtypeStruct((t, D_MODEL), F32),
        compiler_params=_cparams(("parallel",)),
        name="merge",
    )(h, o, z, att, gd, ga, on, wdn, wat, wo)


def _mlp_ple_kernel(h_ref, pe_ref, nm_ref, np_ref, wup_ref, wdown_ref, wpg_ref, wple_ref, out_ref):
    h1 = h_ref[...]
    hm = jnp.dot(_rms(h1, nm_ref[...]).astype(BF16), wup_ref[...], preferred_element_type=F32)
    act = jnp.square(jnp.maximum(hm, 0.0)).astype(BF16)
    h2 = h1 + jnp.dot(act, wdown_ref[...], preferred_element_type=F32)
    gate = _sigmoid(jnp.dot(_rms(h2, np_ref[...]).astype(BF16), wpg_ref[...], preferred_element_type=F32))
    pp = jnp.dot(pe_ref[...].astype(BF16), wple_ref[...], preferred_element_type=F32)
    out_ref[...] = h2 + gate * pp


def _mlp_ple(h1, pe, nm, npl, wup, wdown, wpg, wple, *, tm):
    t = h1.shape[0]
    row = lambda w: pl.BlockSpec((tm, w), lambda i: (i, 0))
    return pl.pallas_call(
        _mlp_ple_kernel,
        grid=(t // tm,),
        in_specs=[row(D_MODEL), row(PLE_DIM), _resident(nm.shape), _resident(npl.shape),
                  _resident(wup.shape), _resident(wdown.shape), _resident(wpg.shape), _resident(wple.shape)],
        out_specs=row(D_MODEL),
        out_shape=jax.ShapeDtypeStruct((t, D_MODEL), F32),
        compiler_params=_cparams(("parallel",)),
        name="mlp_ple",
    )(h1, pe, nm, npl, wup, wdown, wpg, wple)


def _rope_tables(seq):
    pos = jnp.arange(seq)
    r = (pos // GRID_W).astype(F32)
    c = (pos % GRID_W).astype(F32)
    half = HEAD_DIM // 2
    inv = jnp.power(jnp.float32(ROPE_THETA), -jnp.arange(0, half, 2, dtype=F32) / half)
    ar = r[:, None] * inv
    ac = c[:, None] * inv
    cos = jnp.concatenate([jnp.cos(ar), jnp.cos(ar), jnp.cos(ac), jnp.cos(ac)], axis=1)
    sin = jnp.concatenate([-jnp.sin(ar), jnp.sin(ar), -jnp.sin(ac), jnp.sin(ac)], axis=1)
    return cos, sin


def _prepare_weights(norm_mix, w_in, dn_conv_w, dn_a_log, dn_dt_bias, dn_out_norm, w_dn_branch,
                     q_norm, k_norm, w_attn_branch, w_o, norm_mlp, w_up, w_down, norm_ple,
                     w_ple_gate, w_ple):
    bf = lambda w: w.astype(BF16)
    w_in = w_in[0]
    splits = (2 * DN_QK_W + DN_V_W, DN_V_W, 2 * DN_V_HEADS, 2 * DN_V_HEADS, ATT_Q_W, ATT_KV_W, ATT_KV_W,
              D_MODEL, D_MODEL)
    cols, start = [], 0
    for s in splits:
        cols.append(w_in[:, start:start + s])
        start += s
    w_qkv, w_z, w_b, w_a, w_aq, w_ak, w_av, w_gd, w_ga = cols
    pad = jnp.zeros((D_MODEL, LANES - 2 * DN_V_HEADS), F32)
    w_ba = jnp.concatenate([w_b, pad, w_a, pad], axis=1)
    in_ws = tuple(bf(w) for w in (w_qkv[:, :DN_QK_W], w_qkv[:, DN_QK_W:2 * DN_QK_W], w_qkv[:, 2 * DN_QK_W:],
                                  w_z, w_ba, w_aq, w_ak, w_av, w_gd, w_ga))
    conv = dn_conv_w[0]
    lane_pad = lambda x: jnp.pad(x.reshape(1, -1), ((0, 0), (0, LANES - x.size)))
    return dict(
        norm_mix=norm_mix[0].reshape(1, -1), in_ws=in_ws,
        conv_q=conv[:, :DN_QK_W], conv_k=conv[:, DN_QK_W:2 * DN_QK_W], conv_v=conv[:, 2 * DN_QK_W:],
        alog=lane_pad(dn_a_log[0]), dtb=lane_pad(dn_dt_bias[0]),
        out_norm=dn_out_norm[0].reshape(1, -1), w_dn=bf(w_dn_branch[0]),
        q_norm=q_norm[0].reshape(1, -1), k_norm=k_norm[0].reshape(1, -1),
        w_at=bf(w_attn_branch[0]), w_o=bf(w_o[0]),
        norm_mlp=norm_mlp[0].reshape(1, -1), w_up=bf(w_up[0]), w_down=bf(w_down[0]),
        norm_ple=norm_ple[0].reshape(1, -1), w_pg=bf(w_ple_gate[0]), w_ple=bf(w_ple[0]))


def _tiles(seq):
    return dict(tm_proj=min(256, seq), tm_prep=min(256, seq), cb=min(4, seq // DN_CHUNK),
                tq=min(128, seq), tk=min(512, seq), tm_merge=min(256, seq), tm_mlp=min(256, seq))


def _encoder_layer(x, pe, p):
    batch, seq, _ = x.shape
    assert seq % DN_CHUNK == 0 and seq % GRID_W == 0
    t = batch * seq
    tl = _tiles(seq)
    h = x.reshape(t, D_MODEL)
    cos, sin = _rope_tables(seq)
    dq, dk, dv, z, ba, aq, ak, av, gd, ga = _in_proj(
        h, p["norm_mix"], cos, sin, p["q_norm"], p["k_norm"], p["in_ws"], seq=seq, tm=tl["tm_proj"])
    qn, kn, vc, aux = _dn_prep(dq, dk, dv, ba, p["conv_q"], p["conv_k"], p["conv_v"], p["alog"], p["dtb"],
                               seq=seq, tm=tl["tm_prep"])
    o = _delta_rule(qn, kn, vc, aux, batch=batch, seq=seq, cb=tl["cb"])
    att = _attention(aq, ak, av, batch=batch, seq=seq, tq=tl["tq"], tk=tl["tk"])
    h1 = _merge(h, o, z, att, gd, ga, p["out_norm"], p["w_dn"], p["w_at"], p["w_o"], tm=tl["tm_merge"])
    h3 = _mlp_ple(h1, pe.reshape(t, PLE_DIM), p["norm_mlp"], p["norm_ple"], p["w_up"], p["w_down"],
                  p["w_pg"], p["w_ple"], tm=tl["tm_mlp"])
    return h3.reshape(batch, seq, D_MODEL)


@jax.jit
def kernel(x_prompt, x_sample, p_prompt, p_sample, norm_mix, w_in, dn_conv_w, dn_a_log, dn_dt_bias,
           dn_out_norm, w_dn_branch, q_norm, k_norm, w_attn_branch, w_o, norm_mlp, w_up, w_down,
           norm_ple, w_ple_gate, w_ple):
    assert norm_mix.shape[0] == 1, "single-layer trunk"
    p = _prepare_weights(norm_mix, w_in, dn_conv_w, dn_a_log, dn_dt_bias, dn_out_norm, w_dn_branch,
                         q_norm, k_norm, w_attn_branch, w_o, norm_mlp, w_up, w_down, norm_ple,
                         w_ple_gate, w_ple)
    y_prompt = _encoder_layer(x_prompt, p_prompt[0], p)
    y_sample = _encoder_layer(x_sample, p_sample[0], p)
    return (y_prompt, y_sample)
```

```python
import functools
import math

import jax
import jax.numpy as jnp
from jax import lax
from jax.experimental import pallas as pl
from jax.experimental.pallas import tpu as pltpu

F32 = jnp.float32
BF16 = jnp.bfloat16

D_MODEL = 1024
PLE_DIM = 256
D_FF = 4 * D_MODEL
NORM_EPS = 1e-6
GRID_W = 64
DN_QK_HEADS = 4
DN_V_HEADS = 8
DN_DK = 128
DN_DV = 128
DN_CHUNK = 64
DN_CONV = 5
DN_QK_W = DN_QK_HEADS * DN_DK
DN_V_W = DN_V_HEADS * DN_DV
ATT_HEADS = 8
ATT_KV_HEADS = 2
ATT_GROUP = ATT_HEADS // ATT_KV_HEADS
HEAD_DIM = 128
ROPE_THETA = 10000.0
LOG2_E = math.log2(math.e)
ATT_Q_W = ATT_HEADS * HEAD_DIM
ATT_KV_W = ATT_KV_HEADS * HEAD_DIM
ATT_UNROLL = 16

LANES = 128
SUBLANES = 8
VMEM_LIMIT_BYTES = 56 * 1024 * 1024

DN_GROUP = 4
DN_GM = DN_GROUP * DN_CHUNK
AUX_W = LANES
AUX_GC, AUX_BETA, AUX_EGC, AUX_EGL, AUX_EG = 0, 1, 2, 3, 4


def _cparams(semantics):
    return pltpu.CompilerParams(dimension_semantics=semantics, vmem_limit_bytes=VMEM_LIMIT_BYTES)


def _resident(shape):
    nd = len(shape)
    return pl.BlockSpec(shape, lambda *_: (0,) * nd, pipeline_mode=pl.Buffered(1))


def _rms(x, gain):
    ms = jnp.mean(x * x, axis=-1, keepdims=True)
    return x * lax.rsqrt(ms + NORM_EPS) * gain


def _mm(a, b):
    return jnp.dot(a.astype(BF16), b.astype(BF16), preferred_element_type=F32)


def _mm_nt(a, b):
    return lax.dot_general(a.astype(BF16), b.astype(BF16), (((1,), (1,)), ((), ())),
                           preferred_element_type=F32)


def _mm_tn(a, b):
    return lax.dot_general(a.astype(BF16), b.astype(BF16), (((0,), (0,)), ((), ())),
                           preferred_element_type=F32)


def _sigmoid(x):
    return 1.0 / (1.0 + jnp.exp(-x))


def _silu(x):
    return x * _sigmoid(x)


def _swap32(x):
    lane = lax.broadcasted_iota(jnp.int32, x.shape, 1)
    first = (lane & 32) == 0
    return jnp.where(first, pltpu.roll(x, LANES - 32, 1), pltpu.roll(x, 32, 1))


def _norm_rope(x, gain, cos, sin, scale):
    y = _rms(x, gain)
    return (y * cos + _swap32(y) * sin) * scale


def _in_proj_kernel(x_ref, gain_ref, cos_ref, sin_ref, qn_ref, kn_ref,
                    wdq_ref, wdk_ref, wdv_ref, wz_ref, wba_ref, waq_ref, wak_ref, wav_ref,
                    wgd_ref, wga_ref,
                    dq_o, dk_o, dv_o, z_o, ba_o, aq_o, ak_o, av_o, gd_o, ga_o):
    u = _rms(x_ref[...], gain_ref[...]).astype(BF16)

    def proj(w_ref):
        return jnp.dot(u, w_ref[...], preferred_element_type=F32)

    dq_o[...] = proj(wdq_ref)
    dk_o[...] = proj(wdk_ref)
    dv_o[...] = proj(wdv_ref)
    z_o[...] = proj(wz_ref).astype(z_o.dtype)
    ba_o[...] = proj(wba_ref)
    gd_o[...] = proj(wgd_ref).astype(gd_o.dtype)
    ga_o[...] = proj(wga_ref).astype(ga_o.dtype)

    cos = cos_ref[...]
    sin = sin_ref[...]
    q = proj(waq_ref)
    for h in range(ATT_HEADS):
        sl = slice(h * HEAD_DIM, (h + 1) * HEAD_DIM)
        qh = _norm_rope(q[:, sl], qn_ref[...], cos, sin, HEAD_DIM ** -0.5 * LOG2_E)
        aq_o[sl, :] = jnp.transpose(qh).astype(aq_o.dtype)
    k = proj(wak_ref)
    v = proj(wav_ref)
    for h in range(ATT_KV_HEADS):
        sl = slice(h * HEAD_DIM, (h + 1) * HEAD_DIM)
        ak_o[:, sl] = _norm_rope(k[:, sl], kn_ref[...], cos, sin, 1.0).astype(ak_o.dtype)
        av_o[sl, :] = jnp.transpose(v[:, sl]).astype(av_o.dtype)


def _in_proj(x, gain, cos, sin, qn, kn, ws, *, seq, tm):
    t = x.shape[0]
    nseq = seq // tm
    row = lambda w: pl.BlockSpec((tm, w), lambda i: (i, 0))
    tab = pl.BlockSpec((tm, HEAD_DIM), lambda i: (i % nseq, 0))
    col = lambda w: pl.BlockSpec((w, tm), lambda i: (0, i))
    sds = jax.ShapeDtypeStruct
    outs = [(row(DN_QK_W), sds((t, DN_QK_W), F32)), (row(DN_QK_W), sds((t, DN_QK_W), F32)),
            (row(DN_V_W), sds((t, DN_V_W), F32)), (row(DN_V_W), sds((t, DN_V_W), BF16)),
            (row(2 * LANES), sds((t, 2 * LANES), F32)), (col(ATT_Q_W), sds((ATT_Q_W, t), BF16)),
            (row(ATT_KV_W), sds((t, ATT_KV_W), BF16)), (col(ATT_KV_W), sds((ATT_KV_W, t), BF16)),
            (row(D_MODEL), sds((t, D_MODEL), BF16)), (row(D_MODEL), sds((t, D_MODEL), BF16))]
    return pl.pallas_call(
        _in_proj_kernel,
        grid=(t // tm,),
        in_specs=[row(D_MODEL), _resident((1, D_MODEL)), tab, tab,
                  _resident((1, HEAD_DIM)), _resident((1, HEAD_DIM))]
        + [_resident(w.shape) for w in ws],
        out_specs=[o[0] for o in outs],
        out_shape=[o[1] for o in outs],
        compiler_params=_cparams(("parallel",)),
        name="in_proj",
    )(x, gain, cos, sin, qn, kn, *ws)


HALO = SUBLANES


def _conv_silu(cur_ref, prev_ref, nxt_ref, w_ref, ext_ref, first, last, tm):
    width = cur_ref.shape[1]
    ext_ref[0:HALO, 0:width] = jnp.where(first, 0.0, prev_ref[...])
    ext_ref[HALO:HALO + tm, 0:width] = cur_ref[...]
    ext_ref[HALO + tm:HALO + tm + HALO, 0:width] = jnp.where(last, 0.0, nxt_ref[...])
    pad = DN_CONV // 2
    acc = None
    for tap in range(DN_CONV):
        start = HALO - pad + tap
        term = ext_ref[start:start + tm, 0:width] * w_ref[tap:tap + 1, :]
        acc = term if acc is None else acc + term
    return _silu(acc)


def _l2norm_heads(x, scale):
    outs = []
    for h in range(DN_QK_HEADS):
        xh = x[:, h * DN_DK:(h + 1) * DN_DK]
        r = lax.rsqrt(jnp.sum(xh * xh, axis=-1, keepdims=True) + NORM_EPS)
        outs.append(xh * (r * scale))
    return jnp.concatenate(outs, axis=1)


def _split3(x):
    hi = x.astype(BF16)
    r1 = x - hi.astype(F32)
    mid = r1.astype(BF16)
    lo = (r1 - mid.astype(F32)).astype(BF16)
    return hi, mid, lo


def _ones_mm(mask_bf, x):
    hi, mid, lo = _split3(x)
    dot = lambda p: jnp.dot(mask_bf, p, preferred_element_type=F32)
    return dot(hi) + dot(mid) + dot(lo)


def _place(x, src, dst):
    shift = (dst - src) % LANES
    return x if shift == 0 else pltpu.roll(x, shift, 1)


def _dn_prep_kernel(q_ref, qp_ref, qn_ref, k_ref, kp_ref, kn_ref, v_ref, vp_ref, vn_ref,
                    ba_ref, wq_ref, wk_ref, wv_ref, alog_ref, dtb_ref,
                    q_o, k_o, v_o, aux_o, ext_ref, *, tm, nseq):
    i = pl.program_id(0)
    first = (i % nseq) == 0
    last = (i % nseq) == nseq - 1
    q = _conv_silu(q_ref, qp_ref, qn_ref, wq_ref, ext_ref, first, last, tm)
    q_o[...] = _l2norm_heads(q, DN_DK ** -0.5)
    k = _conv_silu(k_ref, kp_ref, kn_ref, wk_ref, ext_ref, first, last, tm)
    k_o[...] = _l2norm_heads(k, 1.0)
    v_o[...] = _conv_silu(v_ref, vp_ref, vn_ref, wv_ref, ext_ref, first, last, tm)

    ba = ba_ref[...]
    beta = _sigmoid(ba[:, 0:LANES])
    a = ba[:, LANES:2 * LANES] + dtb_ref[...]
    softplus = jnp.maximum(a, 0.0) + jnp.log(1.0 + jnp.exp(-jnp.abs(a)))
    g = -jnp.exp(alog_ref[...]) * softplus

    ri = lax.broadcasted_iota(jnp.int32, (tm, tm), 0)
    ci = lax.broadcasted_iota(jnp.int32, (tm, tm), 1)
    same = lax.shift_right_logical(ri, 6) == lax.shift_right_logical(ci, 6)
    one = lambda m: jnp.where(m, 1.0, 0.0).astype(BF16)
    pre = _ones_mm(one(same & (ci <= ri)), g)
    suf = _ones_mm(one(same & (ci >= ri)), g)
    tot = _ones_mm(one(same), g)
    lane = lax.broadcasted_iota(jnp.int32, (tm, LANES), 1)
    gc = jnp.where(lane < DN_V_HEADS, pre, suf)
    egc = jnp.exp(gc)
    egl = jnp.exp(tot - gc)
    eg = jnp.exp(tot)
    for d in range(2):
        src = d * DN_V_HEADS
        aux = jnp.zeros((tm, AUX_W), F32)
        for field, val in ((AUX_GC, gc), (AUX_BETA, beta), (AUX_EGC, egc), (AUX_EGL, egl), (AUX_EG, eg)):
            dst = field * DN_V_HEADS
            aux = jnp.where((lane >= dst) & (lane < dst + DN_V_HEADS), _place(val, src, dst), aux)
        aux_o[d] = aux


def _dn_prep(dq, dk, dv, ba, wq, wk, wv, alog, dtb, *, seq, tm):
    t = dq.shape[0]
    nseq = seq // tm
    hb = tm // HALO
    nhb = t // HALO

    def trio(w):
        return [pl.BlockSpec((tm, w), lambda i: (i, 0)),
                pl.BlockSpec((HALO, w), lambda i: (jnp.maximum(i * hb - 1, 0), 0)),
                pl.BlockSpec((HALO, w), lambda i: (jnp.minimum((i + 1) * hb, nhb - 1), 0))]

    row = lambda w: pl.BlockSpec((tm, w), lambda i: (i, 0))
    return pl.pallas_call(
        functools.partial(_dn_prep_kernel, tm=tm, nseq=nseq),
        grid=(t // tm,),
        in_specs=trio(DN_QK_W) + trio(DN_QK_W) + trio(DN_V_W) + [row(2 * LANES)]
        + [_resident(wq.shape), _resident(wk.shape), _resident(wv.shape),
           _resident(alog.shape), _resident(dtb.shape)],
        out_specs=[row(DN_QK_W), row(DN_QK_W), row(DN_V_W),
                   pl.BlockSpec((2, tm, AUX_W), lambda i: (0, i, 0))],
        out_shape=[jax.ShapeDtypeStruct((t, DN_QK_W), F32), jax.ShapeDtypeStruct((t, DN_QK_W), F32),
                   jax.ShapeDtypeStruct((t, DN_V_W), F32), jax.ShapeDtypeStruct((2, t, AUX_W), F32)],
        scratch_shapes=[pltpu.VMEM((tm + 2 * HALO, DN_V_W), F32)],
        compiler_params=_cparams(("parallel",)),
        name="dn_prep",
    )(dq, dq, dq, dk, dk, dk, dv, dv, dv, ba, wq, wk, wv, alog, dtb)


def _unit_lower_inverses(mats, eye, m16, m32):
    each = lambda f, *ls: [f(*xs) for xs in zip(*ls)]
    a32 = [jnp.where(m32 & ~m16, a, 0.0).astype(BF16) for a in mats]
    a64 = [jnp.where(m32, 0.0, a).astype(BF16) for a in mats]
    n = [jnp.where(m16, -a, 0.0) for a in mats]
    x = [eye + ni for ni in n]
    p = each(_mm, n, n)
    for level in range(3):
        x = each(lambda xi, pi: xi + _mm(xi, pi), x, p)
        if level < 2:
            p = each(_mm, p, p)
    for off in (a32, a64):
        t = each(_mm, off, x)
        x = each(lambda xi, ti: xi - _mm(xi, ti), x, t)
    return x


def _delta_rule_kernel(q_ref, k_ref, v_ref, aux_ref, o_ref, s_ref, wq_ref, u_ref, kd_ref, aqk_ref, *, cb):
    d = pl.program_id(1)
    j = pl.program_id(2)

    @pl.when(j == 0)
    def _():
        s_ref[...] = jnp.zeros_like(s_ref)

    gm = DN_GM
    ri = lax.broadcasted_iota(jnp.int32, (gm, gm), 0)
    ci = lax.broadcasted_iota(jnp.int32, (gm, gm), 1)
    blk = lambda x, s: lax.shift_right_logical(x, s)
    same = blk(ri, 6) == blk(ci, 6)
    rel = (ri - ci) * (1 - 2 * d)
    incl = same & (rel >= 0)
    strict = same & (rel > 0)
    m16 = blk(ri, 4) == blk(ci, 4)
    m32 = blk(ri, 5) == blk(ci, 5)
    eye = jnp.where(ri == ci, 1.0, 0.0).astype(F32)

    ngroups = DN_V_HEADS // DN_GROUP

    def chunk_rows(jj):
        c = jnp.where(d == 0, jj, cb - 1 - jj)
        return pl.ds(pl.multiple_of(c * DN_CHUNK, DN_CHUNK), DN_CHUNK)

    chains = [(jj, hg) for jj in range(cb) for hg in range(ngroups)]
    amats, rhss = [], []
    for jj, hg in chains:
        rows = chunk_rows(jj)
        aux = aux_ref[0, rows, :]
        h0 = hg * DN_GROUP

        def col(field):
            base = field * DN_V_HEADS + h0
            return jnp.concatenate([aux[:, base + r:base + r + 1] for r in range(DN_GROUP)], axis=0)

        gc_col, beta_col, egc_col, egl_col = col(AUX_GC), col(AUX_BETA), col(AUX_EGC), col(AUX_EGL)
        gc_row = jnp.transpose(jnp.broadcast_to(gc_col, (gm, LANES)))[0:1, :]

        qk_heads = [(h0 + r) // (DN_V_HEADS // DN_QK_HEADS) for r in range(DN_GROUP)]
        ks = jnp.concatenate([k_ref[rows, h * DN_DK:(h + 1) * DN_DK] for h in qk_heads], axis=0)
        qs = jnp.concatenate([q_ref[rows, h * DN_DK:(h + 1) * DN_DK] for h in qk_heads], axis=0)
        vs = jnp.concatenate([v_ref[rows, (h0 + r) * DN_DV:(h0 + r + 1) * DN_DV]
                              for r in range(DN_GROUP)], axis=0)

        gram = _mm_nt(ks, ks)
        qk = _mm_nt(qs, ks)
        decay = jnp.where(incl, jnp.exp(jnp.where(incl, gc_col - gc_row, 0.0)), 0.0)
        amats.append(jnp.where(strict, beta_col * gram * decay, 0.0))
        aqk_ref[jj, hg] = (qk * decay).astype(BF16)
        rhss.append(jnp.concatenate([vs * beta_col, ks * (beta_col * egc_col)], axis=1).astype(BF16))
        kd_ref[jj, hg] = (ks * egl_col).astype(BF16)
        qd = (qs * egc_col).astype(BF16)
        for r in range(DN_GROUP):
            wq_ref[jj, h0 + r, DN_CHUNK:2 * DN_CHUNK, :] = qd[r * DN_CHUNK:(r + 1) * DN_CHUNK]

    tinvs = _unit_lower_inverses(amats, eye, m16, m32)
    for (jj, hg), tinv, rhs in zip(chains, tinvs, rhss):
        sol = _mm(tinv, rhs)
        u_ref[jj, hg] = sol[:, 0:DN_DV]
        w = sol[:, DN_DV:DN_DV + DN_DK].astype(BF16)
        for r in range(DN_GROUP):
            wq_ref[jj, hg * DN_GROUP + r, 0:DN_CHUNK, :] = w[r * DN_CHUNK:(r + 1) * DN_CHUNK]

    for jj in range(cb):
        rows = chunk_rows(jj)
        eg_row = aux_ref[0, pl.ds(rows.start, 1), :]
        hslice = lambda h: slice(h * DN_DV, (h + 1) * DN_DV)
        states = [s_ref[:, hslice(h)] for h in range(DN_V_HEADS)]
        res = [jnp.dot(wq_ref[jj, h], states[h].astype(BF16), preferred_element_type=F32)
               for h in range(DN_V_HEADS)]
        vnew = []
        for h in range(DN_V_HEADS):
            hg, r = divmod(h, DN_GROUP)
            vnew.append(u_ref[jj, hg, r * DN_CHUNK:(r + 1) * DN_CHUNK, :] - res[h][0:DN_CHUNK])
        for h in range(DN_V_HEADS):
            hg, r = divmod(h, DN_GROUP)
            eg = eg_row[:, AUX_EG * DN_V_HEADS + h:AUX_EG * DN_V_HEADS + h + 1]
            kd = kd_ref[jj, hg, r * DN_CHUNK:(r + 1) * DN_CHUNK, :]
            s_ref[:, hslice(h)] = states[h] * eg + _mm_tn(kd, vnew[h])
        for hg in range(ngroups):
            hs = range(hg * DN_GROUP, (hg + 1) * DN_GROUP)
            out = (jnp.concatenate([res[h][DN_CHUNK:2 * DN_CHUNK] for h in hs], axis=0)
                   + _mm(aqk_ref[jj, hg], jnp.concatenate([vnew[h] for h in hs], axis=0)))
            for r, h in enumerate(hs):
                o_ref[0, rows, hslice(h)] = out[r * DN_CHUNK:(r + 1) * DN_CHUNK]


def _delta_rule(q, k, v, aux, *, batch, seq, cb):
    t = q.shape[0]
    rb = cb * DN_CHUNK
    nb = seq // rb
    ngroups = DN_V_HEADS // DN_GROUP

    def rblock(b, d, j):
        return b * nb + jnp.where(d == 0, j, nb - 1 - j)

    return pl.pallas_call(
        functools.partial(_delta_rule_kernel, cb=cb),
        grid=(batch, 2, nb),
        in_specs=[pl.BlockSpec((rb, DN_QK_W), lambda b, d, j: (rblock(b, d, j), 0)),
                  pl.BlockSpec((rb, DN_QK_W), lambda b, d, j: (rblock(b, d, j), 0)),
                  pl.BlockSpec((rb, DN_V_W), lambda b, d, j: (rblock(b, d, j), 0)),
                  pl.BlockSpec((1, rb, AUX_W), lambda b, d, j: (d, rblock(b, d, j), 0))],
        out_specs=pl.BlockSpec((1, rb, DN_V_W), lambda b, d, j: (d, rblock(b, d, j), 0)),
        out_shape=jax.ShapeDtypeStruct((2, t, DN_V_W), F32),
        scratch_shapes=[pltpu.VMEM((DN_DK, DN_V_HEADS * DN_DV), F32),
                        pltpu.VMEM((cb, DN_V_HEADS, 2 * DN_CHUNK, DN_DK), BF16),
                        pltpu.VMEM((cb, ngroups, DN_GM, DN_DV), F32),
                        pltpu.VMEM((cb, ngroups, DN_GM, DN_DK), BF16),
                        pltpu.VMEM((cb, ngroups, DN_GM, DN_GM), BF16)],
        compiler_params=_cparams(("parallel", "parallel", "arbitrary")),
        name="delta_rule",
    )(q, k, v, aux)


def _attention_kernel(qt_ref, k_ref, vt_ref, o_ref, acc_ref, m_ref, l_ref, *, tk, seq):
    m_ref[...] = jnp.full(m_ref.shape, -jnp.inf, F32)
    l_ref[...] = jnp.zeros(l_ref.shape, F32)
    acc_ref[...] = jnp.zeros(acc_ref.shape, F32)

    def step(c, carry):
        ks = pl.ds(pl.multiple_of(c * tk, tk), tk)
        kc = k_ref[ks, :]
        vtc = vt_ref[:, ks]
        for g in range(ATT_GROUP):
            s = jnp.dot(kc, qt_ref[g * HEAD_DIM:(g + 1) * HEAD_DIM, :], preferred_element_type=F32)
            m_old = m_ref[g:g + 1, :]
            m_new = jnp.maximum(m_old, jnp.max(s, axis=0, keepdims=True))
            alpha = jnp.exp2(m_old - m_new)
            p = jnp.exp2(s - m_new)
            l_ref[g:g + 1, :] = alpha * l_ref[g:g + 1, :] + jnp.sum(p, axis=0, keepdims=True)
            m_ref[g:g + 1, :] = m_new
            acc_ref[g] = alpha * acc_ref[g] + jnp.dot(vtc, p.astype(BF16), preferred_element_type=F32)
        return carry

    nsteps = seq // tk
    lax.fori_loop(0, nsteps, step, 0, unroll=math.gcd(nsteps, ATT_UNROLL))
    for g in range(ATT_GROUP):
        out = acc_ref[g] / l_ref[g:g + 1, :]
        o_ref[:, g * HEAD_DIM:(g + 1) * HEAD_DIM] = jnp.transpose(out).astype(o_ref.dtype)


def _attention(aqt, ak, avt, *, batch, seq, tq, tk):
    t = ak.shape[0]
    nq = seq // tq
    gw = ATT_GROUP * HEAD_DIM
    return pl.pallas_call(
        functools.partial(_attention_kernel, tk=tk, seq=seq),
        grid=(batch, ATT_KV_HEADS, nq),
        in_specs=[pl.BlockSpec((gw, tq), lambda b, g, i: (g, b * nq + i)),
                  pl.BlockSpec((seq, HEAD_DIM), lambda b, g, i: (b, g)),
                  pl.BlockSpec((HEAD_DIM, seq), lambda b, g, i: (g, b))],
        out_specs=pl.BlockSpec((tq, gw), lambda b, g, i: (b * nq + i, g)),
        out_shape=jax.ShapeDtypeStruct((t, ATT_Q_W), BF16),
        scratch_shapes=[pltpu.VMEM((ATT_GROUP, HEAD_DIM, tq), F32), pltpu.VMEM((SUBLANES, tq), F32),
                        pltpu.VMEM((SUBLANES, tq), F32)],
        compiler_params=_cparams(("parallel", "parallel", "parallel")),
        name="attention",
    )(aqt, ak, avt)


def _merge_kernel(h_ref, o_ref, z_ref, att_ref, gd_ref, ga_ref, on_ref, wdn_ref, wat_ref, wo_ref, out_ref):
    o = o_ref[0] + o_ref[1]
    z = z_ref[...].astype(F32)
    parts = []
    for h in range(DN_V_HEADS):
        sl = slice(h * DN_DV, (h + 1) * DN_DV)
        parts.append((_rms(o[:, sl], on_ref[...]) * _silu(z[:, sl])).astype(BF16))
    o_dn = jnp.concatenate(parts, axis=1)
    y_dn = jnp.dot(o_dn, wdn_ref[...], preferred_element_type=F32)
    y_at = jnp.dot(att_ref[...], wat_ref[...], preferred_element_type=F32)
    merged = _sigmoid(gd_ref[...].astype(F32)) * y_dn + _sigmoid(ga_ref[...].astype(F32)) * y_at
    out_ref[...] = h_ref[...] + jnp.dot(merged.astype(BF16), wo_ref[...], preferred_element_type=F32)


def _merge(h, o, z, att, gd, ga, on, wdn, wat, wo, *, tm):
    t = h.shape[0]
    row = lambda w: pl.BlockSpec((tm, w), lambda i: (i, 0))
    return pl.pallas_call(
        _merge_kernel,
        grid=(t // tm,),
        in_specs=[row(D_MODEL), pl.BlockSpec((2, tm, DN_V_W), lambda i: (0, i, 0)), row(DN_V_W),
                  row(ATT_Q_W), row(D_MODEL), row(D_MODEL), _resident(on.shape),
                  _resident(wdn.shape), _resident(wat.shape), _resident(wo.shape)],
        out_specs=row(D_MODEL),
        out_shape=jax.ShapeDtypeStruct((t, D_MODEL), F32),
        compiler_params=_cparams(("parallel",)),
        name="merge",
    )(h, o, z, att, gd, ga, on, wdn, wat, wo)


def _mlp_ple_kernel(h_ref, pe_ref, nm_ref, np_ref, wup_ref, wdown_ref, wpg_ref, wple_ref, out_ref):
    h1 = h_ref[...]
    hm = jnp.dot(_rms(h1, nm_ref[...]).astype(BF16), wup_ref[...], preferred_element_type=F32)
    act = jnp.square(jnp.maximum(hm, 0.0)).astype(BF16)
    h2 = h1 + jnp.dot(act, wdown_ref[...], preferred_element_type=F32)
    gate = _sigmoid(jnp.dot(_rms(h2, np_ref[...]).astype(BF16), wpg_ref[...], preferred_element_type=F32))
    pp = jnp.dot(pe_ref[...].astype(BF16), wple_ref[...], preferred_element_type=F32)
    out_ref[...] = h2 + gate * pp


def _mlp_ple(h1, pe, nm, npl, wup, wdown, wpg, wple, *, tm):
    t = h1.shape[0]
    row = lambda w: pl.BlockSpec((tm, w), lambda i: (i, 0))
    return pl.pallas_call(
        _mlp_ple_kernel,
        grid=(t // tm,),
        in_specs=[row(D_MODEL), row(PLE_DIM), _resident(nm.shape), _resident(npl.shape),
                  _resident(wup.shape), _resident(wdown.shape), _resident(wpg.shape), _resident(wple.shape)],
        out_specs=row(D_MODEL),
        out_shape=jax.ShapeDtypeStruct((t, D_MODEL), F32),
        compiler_params=_cparams(("parallel",)),
        name="mlp_ple",
    )(h1, pe, nm, npl, wup, wdown, wpg, wple)


def _rope_tables(seq):
    pos = jnp.arange(seq)
    r = (pos // GRID_W).astype(F32)
    c = (pos % GRID_W).astype(F32)
    half = HEAD_DIM // 2
    inv = jnp.power(jnp.float32(ROPE_THETA), -jnp.arange(0, half, 2, dtype=F32) / half)
    ar = r[:, None] * inv
    ac = c[:, None] * inv
    cos = jnp.concatenate([jnp.cos(ar), jnp.cos(ar), jnp.cos(ac), jnp.cos(ac)], axis=1)
    sin = jnp.concatenate([-jnp.sin(ar), jnp.sin(ar), -jnp.sin(ac), jnp.sin(ac)], axis=1)
    return cos, sin


def _prepare_weights(norm_mix, w_in, dn_conv_w, dn_a_log, dn_dt_bias, dn_out_norm, w_dn_branch,
                     q_norm, k_norm, w_attn_branch, w_o, norm_mlp, w_up, w_down, norm_ple,
                     w_ple_gate, w_ple):
    bf = lambda w: w.astype(BF16)
    w_in = w_in[0]
    splits = (2 * DN_QK_W + DN_V_W, DN_V_W, 2 * DN_V_HEADS, 2 * DN_V_HEADS, ATT_Q_W, ATT_KV_W, ATT_KV_W,
              D_MODEL, D_MODEL)
    cols, start = [], 0
    for s in splits:
        cols.append(w_in[:, start:start + s])
        start += s
    w_qkv, w_z, w_b, w_a, w_aq, w_ak, w_av, w_gd, w_ga = cols
    pad = jnp.zeros((D_MODEL, LANES - 2 * DN_V_HEADS), F32)
    w_ba = jnp.concatenate([w_b, pad, w_a, pad], axis=1)
    in_ws = tuple(bf(w) for w in (w_qkv[:, :DN_QK_W], w_qkv[:, DN_QK_W:2 * DN_QK_W], w_qkv[:, 2 * DN_QK_W:],
                                  w_z, w_ba, w_aq, w_ak, w_av, w_gd, w_ga))
    conv = dn_conv_w[0]
    lane_pad = lambda x: jnp.pad(x.reshape(1, -1), ((0, 0), (0, LANES - x.size)))
    return dict(
        norm_mix=norm_mix[0].reshape(1, -1), in_ws=in_ws,
        conv_q=conv[:, :DN_QK_W], conv_k=conv[:, DN_QK_W:2 * DN_QK_W], conv_v=conv[:, 2 * DN_QK_W:],
        alog=lane_pad(dn_a_log[0]), dtb=lane_pad(dn_dt_bias[0]),
        out_norm=dn_out_norm[0].reshape(1, -1), w_dn=bf(w_dn_branch[0]),
        q_norm=q_norm[0].reshape(1, -1), k_norm=k_norm[0].reshape(1, -1),
        w_at=bf(w_attn_branch[0]), w_o=bf(w_o[0]),
        norm_mlp=norm_mlp[0].reshape(1, -1), w_up=bf(w_up[0]), w_down=bf(w_down[0]),
        norm_ple=norm_ple[0].reshape(1, -1), w_pg=bf(w_ple_gate[0]), w_ple=bf(w_ple[0]))


def _tiles(seq):
    return dict(tm_proj=min(256, seq), tm_prep=min(256, seq), cb=min(4, seq // DN_CHUNK),
                tq=min(256, seq), tk=min(128, seq), tm_merge=min(256, seq), tm_mlp=min(256, seq))


def _encoder_layer(x, pe, p):
    batch, seq, _ = x.shape
    assert seq % DN_CHUNK == 0 and seq % GRID_W == 0
    t = batch * seq
    tl = _tiles(seq)
    h = x.reshape(t, D_MODEL)
    cos, sin = _rope_tables(seq)
    dq, dk, dv, z, ba, aq, ak, av, gd, ga = _in_proj(
        h, p["norm_mix"], cos, sin, p["q_norm"], p["k_norm"], p["in_ws"], seq=seq, tm=tl["tm_proj"])
    qn, kn, vc, aux = _dn_prep(dq, dk, dv, ba, p["conv_q"], p["conv_k"], p["conv_v"], p["alog"], p["dtb"],
                               seq=seq, tm=tl["tm_prep"])
    o = _delta_rule(qn, kn, vc, aux, batch=batch, seq=seq, cb=tl["cb"])
    att = _attention(aq, ak, av, batch=batch, seq=seq, tq=tl["tq"], tk=tl["tk"])
    h1 = _merge(h, o, z, att, gd, ga, p["out_norm"], p["w_dn"], p["w_at"], p["w_o"], tm=tl["tm_merge"])
    h3 = _mlp_ple(h1, pe.reshape(t, PLE_DIM), p["norm_mlp"], p["norm_ple"], p["w_up"], p["w_down"],
                  p["w_pg"], p["w_ple"], tm=tl["tm_mlp"])
    return h3.reshape(batch, seq, D_MODEL)


@jax.jit
def kernel(x_prompt, x_sample, p_prompt, p_sample, norm_mix, w_in, dn_conv_w, dn_a_log, dn_dt_bias,
           dn_out_norm, w_dn_branch, q_norm, k_norm, w_attn_branch, w_o, norm_mlp, w_up, w_down,
           norm_ple, w_ple_gate, w_ple):
    assert norm_mix.shape[0] == 1, "single-layer trunk"
    p = _prepare_weights(norm_mix, w_in, dn_conv_w, dn_a_log, dn_dt_bias, dn_out_norm, w_dn_branch,
                         q_norm, k_norm, w_attn_branch, w_o, norm_mlp, w_up, w_down, norm_ple,
                         w_ple_gate, w_ple)
    y_prompt = _encoder_layer(x_prompt, p_prompt[0], p)
    y_sample = _encoder_layer(x_sample, p_sample[0], p)
    return (y_prompt, y_sample)
```

```python
import functools
import math

import jax
import jax.numpy as jnp
from jax import lax
from jax.experimental import pallas as pl
from jax.experimental.pallas import tpu as pltpu

F32 = jnp.float32
BF16 = jnp.bfloat16

D_MODEL = 1024
PLE_DIM = 256
D_FF = 4 * D_MODEL
NORM_EPS = 1e-6
GRID_W = 64
DN_QK_HEADS = 4
DN_V_HEADS = 8
DN_DK = 128
DN_DV = 128
DN_CHUNK = 64
DN_CONV = 5
DN_QK_W = DN_QK_HEADS * DN_DK
DN_V_W = DN_V_HEADS * DN_DV
ATT_HEADS = 8
ATT_KV_HEADS = 2
ATT_GROUP = ATT_HEADS // ATT_KV_HEADS
HEAD_DIM = 128
ROPE_THETA = 10000.0
LOG2_E = math.log2(math.e)
ATT_Q_W = ATT_HEADS * HEAD_DIM
ATT_KV_W = ATT_KV_HEADS * HEAD_DIM
ATT_UNROLL = 16

LANES = 128
SUBLANES = 8
VMEM_LIMIT_BYTES = 56 * 1024 * 1024

DN_GROUP = 4
DN_GM = DN_GROUP * DN_CHUNK
DN_PIPE = 2
AUX_W = LANES
AUX_GC, AUX_BETA, AUX_EGC, AUX_EGL, AUX_EG = 0, 1, 2, 3, 4


def _cparams(semantics):
    return pltpu.CompilerParams(dimension_semantics=semantics, vmem_limit_bytes=VMEM_LIMIT_BYTES)


def _resident(shape):
    nd = len(shape)
    return pl.BlockSpec(shape, lambda *_: (0,) * nd, pipeline_mode=pl.Buffered(1))


def _rms(x, gain):
    ms = jnp.mean(x * x, axis=-1, keepdims=True)
    return x * lax.rsqrt(ms + NORM_EPS) * gain


def _mm(a, b):
    return jnp.dot(a.astype(BF16), b.astype(BF16), preferred_element_type=F32)


def _mm_nt(a, b):
    return lax.dot_general(a.astype(BF16), b.astype(BF16), (((1,), (1,)), ((), ())),
                           preferred_element_type=F32)


def _mm_tn(a, b):
    return lax.dot_general(a.astype(BF16), b.astype(BF16), (((0,), (0,)), ((), ())),
                           preferred_element_type=F32)


def _sigmoid(x):
    return 1.0 / (1.0 + jnp.exp(-x))


def _silu(x):
    return x * _sigmoid(x)


def _swap32(x):
    lane = lax.broadcasted_iota(jnp.int32, x.shape, 1)
    first = (lane & 32) == 0
    return jnp.where(first, pltpu.roll(x, LANES - 32, 1), pltpu.roll(x, 32, 1))


def _norm_rope(x, gain, cos, sin, scale):
    y = _rms(x, gain)
    return (y * cos + _swap32(y) * sin) * scale


HALO = 16


def _conv_silu(ext_ref, w_ref, col0, width, tm):
    pad = DN_CONV // 2
    acc = None
    for tap in range(DN_CONV):
        start = HALO - pad + tap
        term = ext_ref[start:start + tm, col0:col0 + width] * w_ref[tap:tap + 1, col0:col0 + width]
        acc = term if acc is None else acc + term
    return _silu(acc)


def _l2norm_heads(x, scale):
    outs = []
    for h in range(DN_QK_HEADS):
        xh = x[:, h * DN_DK:(h + 1) * DN_DK]
        r = lax.rsqrt(jnp.sum(xh * xh, axis=-1, keepdims=True) + NORM_EPS)
        outs.append(xh * (r * scale))
    return jnp.concatenate(outs, axis=1)


def _split3(x):
    hi = x.astype(BF16)
    r1 = x - hi.astype(F32)
    mid = r1.astype(BF16)
    lo = (r1 - mid.astype(F32)).astype(BF16)
    return hi, mid, lo


def _ones_mm(mask_bf, x):
    hi, mid, lo = _split3(x)
    dot = lambda p: jnp.dot(mask_bf, p, preferred_element_type=F32)
    return dot(hi) + dot(mid) + dot(lo)


def _place(x, src, dst):
    shift = (dst - src) % LANES
    return x if shift == 0 else pltpu.roll(x, shift, 1)


def _gate_fields(ba, alog, dtb, aux_o, tm):
    beta = _sigmoid(ba[:, 0:LANES])
    a = ba[:, LANES:2 * LANES] + dtb
    softplus = jnp.maximum(a, 0.0) + jnp.log(1.0 + jnp.exp(-jnp.abs(a)))
    g = -jnp.exp(alog) * softplus

    ri = lax.broadcasted_iota(jnp.int32, (tm, tm), 0)
    ci = lax.broadcasted_iota(jnp.int32, (tm, tm), 1)
    same = lax.shift_right_logical(ri, 6) == lax.shift_right_logical(ci, 6)
    one = lambda m: jnp.where(m, 1.0, 0.0).astype(BF16)
    pre = _ones_mm(one(same & (ci <= ri)), g)
    suf = _ones_mm(one(same & (ci >= ri)), g)
    tot = _ones_mm(one(same), g)
    lane = lax.broadcasted_iota(jnp.int32, (tm, LANES), 1)
    gc = jnp.where(lane < DN_V_HEADS, pre, suf)
    egc = jnp.exp(gc)
    egl = jnp.exp(tot - gc)
    eg = jnp.exp(tot)
    for d in range(2):
        src = d * DN_V_HEADS
        aux = jnp.zeros((tm, AUX_W), F32)
        for field, val in ((AUX_GC, gc), (AUX_BETA, beta), (AUX_EGC, egc), (AUX_EGL, egl), (AUX_EG, eg)):
            dst = field * DN_V_HEADS
            aux = jnp.where((lane >= dst) & (lane < dst + DN_V_HEADS), _place(val, src, dst), aux)
        aux_o[d] = aux


def _in_proj_kernel(x_ref, xp_ref, xn_ref, gain_ref, cos_ref, sin_ref, qn_ref, kn_ref,
                    wqkv_ref, wz_ref, wba_ref, waq_ref, wak_ref, wav_ref, wgd_ref, wga_ref,
                    conv_ref, alog_ref, dtb_ref,
                    q_o, k_o, v_o, aux_o, z_o, aq_o, ak_o, av_o, gd_o, ga_o, ext_ref, *, tm, nseq):
    i = pl.program_id(0)
    first = (i % nseq) == 0
    last = (i % nseq) == nseq - 1
    gain = gain_ref[...]
    norm = lambda ref: _rms(ref[...], gain).astype(BF16)
    u = norm(x_ref)

    def proj(w_ref):
        return jnp.dot(u, w_ref[...], preferred_element_type=F32)

    u_ext = jnp.concatenate([norm(xp_ref), u, norm(xn_ref)], axis=0)
    ext_ref[...] = jnp.dot(u_ext, wqkv_ref[...], preferred_element_type=F32)
    ext_ref[0:HALO, :] = jnp.where(first, 0.0, ext_ref[0:HALO, :])
    ext_ref[HALO + tm:, :] = jnp.where(last, 0.0, ext_ref[HALO + tm:, :])
    q = _conv_silu(ext_ref, conv_ref, 0, DN_QK_W, tm)
    q_o[...] = _l2norm_heads(q, DN_DK ** -0.5).astype(q_o.dtype)
    k = _conv_silu(ext_ref, conv_ref, DN_QK_W, DN_QK_W, tm)
    k_o[...] = _l2norm_heads(k, 1.0).astype(k_o.dtype)
    v_o[...] = _conv_silu(ext_ref, conv_ref, 2 * DN_QK_W, DN_V_W, tm).astype(v_o.dtype)
    _gate_fields(proj(wba_ref), alog_ref[...], dtb_ref[...], aux_o, tm)

    z_o[...] = proj(wz_ref).astype(z_o.dtype)
    gd_o[...] = proj(wgd_ref).astype(gd_o.dtype)
    ga_o[...] = proj(wga_ref).astype(ga_o.dtype)

    cos = cos_ref[...]
    sin = sin_ref[...]
    aq = proj(waq_ref)
    for h in range(ATT_HEADS):
        sl = slice(h * HEAD_DIM, (h + 1) * HEAD_DIM)
        qh = _norm_rope(aq[:, sl], qn_ref[...], cos, sin, HEAD_DIM ** -0.5 * LOG2_E)
        aq_o[sl, :] = jnp.transpose(qh).astype(aq_o.dtype)
    ak = proj(wak_ref)
    av = proj(wav_ref)
    for h in range(ATT_KV_HEADS):
        sl = slice(h * HEAD_DIM, (h + 1) * HEAD_DIM)
        ak_o[:, sl] = _norm_rope(ak[:, sl], kn_ref[...], cos, sin, 1.0).astype(ak_o.dtype)
        av_o[sl, :] = jnp.transpose(av[:, sl]).astype(av_o.dtype)


def _in_proj(x, gain, cos, sin, qn, kn, ws, conv, alog, dtb, *, seq, tm):
    t = x.shape[0]
    nseq = seq // tm
    hb = tm // HALO
    nhb = t // HALO
    row = lambda w: pl.BlockSpec((tm, w), lambda i: (i, 0))
    tab = pl.BlockSpec((tm, HEAD_DIM), lambda i: (i % nseq, 0))
    col = lambda w: pl.BlockSpec((w, tm), lambda i: (0, i))
    sds = jax.ShapeDtypeStruct
    outs = [(row(DN_QK_W), sds((t, DN_QK_W), BF16)), (row(DN_QK_W), sds((t, DN_QK_W), BF16)),
            (row(DN_V_W), sds((t, DN_V_W), BF16)),
            (pl.BlockSpec((2, tm, AUX_W), lambda i: (0, i, 0)), sds((2, t, AUX_W), F32)),
            (row(DN_V_W), sds((t, DN_V_W), BF16)), (col(ATT_Q_W), sds((ATT_Q_W, t), BF16)),
            (row(ATT_KV_W), sds((t, ATT_KV_W), BF16)), (col(ATT_KV_W), sds((ATT_KV_W, t), BF16)),
            (row(D_MODEL), sds((t, D_MODEL), BF16)), (row(D_MODEL), sds((t, D_MODEL), BF16))]
    return pl.pallas_call(
        functools.partial(_in_proj_kernel, tm=tm, nseq=nseq),
        grid=(t // tm,),
        in_specs=[row(D_MODEL),
                  pl.BlockSpec((HALO, D_MODEL), lambda i: (jnp.maximum(i * hb - 1, 0), 0)),
                  pl.BlockSpec((HALO, D_MODEL), lambda i: (jnp.minimum((i + 1) * hb, nhb - 1), 0)),
                  _resident((1, D_MODEL)), tab, tab, _resident((1, HEAD_DIM)), _resident((1, HEAD_DIM))]
        + [_resident(w.shape) for w in ws]
        + [_resident(conv.shape), _resident(alog.shape), _resident(dtb.shape)],
        out_specs=[o[0] for o in outs],
        out_shape=[o[1] for o in outs],
        scratch_shapes=[pltpu.VMEM((tm + 2 * HALO, 2 * DN_QK_W + DN_V_W), F32)],
        compiler_params=_cparams(("parallel",)),
        name="in_proj",
    )(x, x, x, gain, cos, sin, qn, kn, *ws, conv, alog, dtb)


def _unit_lower_inverses(mats, eye, m16, m32):
    each = lambda f, *ls: [f(*xs) for xs in zip(*ls)]
    a32 = [jnp.where(m32 & ~m16, a, 0.0).astype(BF16) for a in mats]
    a64 = [jnp.where(m32, 0.0, a).astype(BF16) for a in mats]
    n = [jnp.where(m16, -a, 0.0) for a in mats]
    x = [eye + ni for ni in n]
    p = each(_mm, n, n)
    yield
    for level in range(3):
        x = each(lambda xi, pi: xi + _mm(xi, pi), x, p)
        if level < 2:
            p = each(_mm, p, p)
        yield
    for off in (a32, a64):
        t = each(_mm, off, x)
        yield
        x = each(lambda xi, ti: xi - _mm(xi, ti), x, t)
        yield
    return x


def _interleave(*gens):
    live = list(gens)
    while live:
        for g in list(live):
            try:
                next(g)
            except StopIteration:
                live.remove(g)


def _delta_rule_kernel(q_ref, k_ref, v_ref, aux_ref, o_ref, s_ref, wq_ref, u_ref, kd_ref, aqk_ref, *, cb):
    d = pl.program_id(1)
    j = pl.program_id(2)

    @pl.when(j == 0)
    def _():
        s_ref[...] = jnp.zeros_like(s_ref)

    gm = DN_GM
    ri = lax.broadcasted_iota(jnp.int32, (gm, gm), 0)
    ci = lax.broadcasted_iota(jnp.int32, (gm, gm), 1)
    blk = lambda x, s: lax.shift_right_logical(x, s)
    same = blk(ri, 6) == blk(ci, 6)
    rel = (ri - ci) * (1 - 2 * d)
    incl = same & (rel >= 0)
    strict = same & (rel > 0)
    m16 = blk(ri, 4) == blk(ci, 4)
    m32 = blk(ri, 5) == blk(ci, 5)
    eye = jnp.where(ri == ci, 1.0, 0.0).astype(F32)

    ngroups = DN_V_HEADS // DN_GROUP

    def chunk_rows(jj):
        c = jnp.where(d == 0, jj, cb - 1 - jj)
        return pl.ds(pl.multiple_of(c * DN_CHUNK, DN_CHUNK), DN_CHUNK)

    f32 = lambda x: x.astype(F32)

    def local_work(jjs):
        chains = [(jj, hg) for jj in jjs for hg in range(ngroups)]
        amats, rhss = [], []
        for jj, hg in chains:
            rows = chunk_rows(jj)
            aux = aux_ref[0, rows, :]
            h0 = hg * DN_GROUP

            def col(field):
                base = field * DN_V_HEADS + h0
                return jnp.concatenate([aux[:, base + r:base + r + 1] for r in range(DN_GROUP)], axis=0)

            gc_col, beta_col, egc_col, egl_col = col(AUX_GC), col(AUX_BETA), col(AUX_EGC), col(AUX_EGL)
            gc_row = jnp.transpose(jnp.broadcast_to(gc_col, (gm, LANES)))[0:1, :]

            qk_heads = [(h0 + r) // (DN_V_HEADS // DN_QK_HEADS) for r in range(DN_GROUP)]
            ks = jnp.concatenate([k_ref[rows, h * DN_DK:(h + 1) * DN_DK] for h in qk_heads], axis=0)
            qs = jnp.concatenate([q_ref[rows, h * DN_DK:(h + 1) * DN_DK] for h in qk_heads], axis=0)
            vs = jnp.concatenate([v_ref[rows, (h0 + r) * DN_DV:(h0 + r + 1) * DN_DV]
                                  for r in range(DN_GROUP)], axis=0)

            gram = _mm_nt(ks, ks)
            qk = _mm_nt(qs, ks)
            decay = jnp.where(incl, jnp.exp(jnp.where(incl, gc_col - gc_row, 0.0)), 0.0)
            amats.append(jnp.where(strict, beta_col * gram * decay, 0.0))
            aqk_ref[jj, hg] = (qk * decay).astype(BF16)
            rhss.append(jnp.concatenate([f32(vs) * beta_col, f32(ks) * (beta_col * egc_col)],
                                        axis=1).astype(BF16))
            kd_ref[jj, hg] = (f32(ks) * egl_col).astype(BF16)
            qd = (f32(qs) * egc_col).astype(BF16)
            for r in range(DN_GROUP):
                wq_ref[jj, h0 + r, DN_CHUNK:2 * DN_CHUNK, :] = qd[r * DN_CHUNK:(r + 1) * DN_CHUNK]
        yield
        tinvs = yield from _unit_lower_inverses(amats, eye, m16, m32)
        for (jj, hg), tinv, rhs in zip(chains, tinvs, rhss):
            sol = _mm(tinv, rhs)
            u_ref[jj, hg] = sol[:, 0:DN_DV]
            w = sol[:, DN_DV:DN_DV + DN_DK].astype(BF16)
            for r in range(DN_GROUP):
                wq_ref[jj, hg * DN_GROUP + r, 0:DN_CHUNK, :] = w[r * DN_CHUNK:(r + 1) * DN_CHUNK]
        yield

    def recurrence(jjs):
        hslice = lambda h: slice(h * DN_DV, (h + 1) * DN_DV)
        for jj in jjs:
            rows = chunk_rows(jj)
            eg_row = aux_ref[0, pl.ds(rows.start, 1), :]
            states = [s_ref[:, hslice(h)] for h in range(DN_V_HEADS)]
            res = [jnp.dot(wq_ref[jj, h], states[h].astype(BF16), preferred_element_type=F32)
                   for h in range(DN_V_HEADS)]
            yield
            vnew = []
            for h in range(DN_V_HEADS):
                hg, r = divmod(h, DN_GROUP)
                vnew.append(u_ref[jj, hg, r * DN_CHUNK:(r + 1) * DN_CHUNK, :] - res[h][0:DN_CHUNK])
            for h in range(DN_V_HEADS):
                hg, r = divmod(h, DN_GROUP)
                eg = eg_row[:, AUX_EG * DN_V_HEADS + h:AUX_EG * DN_V_HEADS + h + 1]
                kd = kd_ref[jj, hg, r * DN_CHUNK:(r + 1) * DN_CHUNK, :]
                s_ref[:, hslice(h)] = states[h] * eg + _mm_tn(kd, vnew[h])
            yield
            for hg in range(ngroups):
                hs = range(hg * DN_GROUP, (hg + 1) * DN_GROUP)
                out = (jnp.concatenate([res[h][DN_CHUNK:2 * DN_CHUNK] for h in hs], axis=0)
                       + _mm(aqk_ref[jj, hg], jnp.concatenate([vnew[h] for h in hs], axis=0)))
                for r, h in enumerate(hs):
                    o_ref[0, rows, hslice(h)] = out[r * DN_CHUNK:(r + 1) * DN_CHUNK].astype(o_ref.dtype)
            yield

    stages = [range(i, min(i + DN_PIPE, cb)) for i in range(0, cb, DN_PIPE)]
    _interleave(local_work(stages[0]))
    for prev, cur in zip(stages, stages[1:]):
        _interleave(local_work(cur), recurrence(prev))
    _interleave(recurrence(stages[-1]))


def _delta_rule(q, k, v, aux, *, batch, seq, cb):
    t = q.shape[0]
    rb = cb * DN_CHUNK
    nb = seq // rb
    ngroups = DN_V_HEADS // DN_GROUP

    def rblock(b, d, j):
        return b * nb + jnp.where(d == 0, j, nb - 1 - j)

    return pl.pallas_call(
        functools.partial(_delta_rule_kernel, cb=cb),
        grid=(batch, 2, nb),
        in_specs=[pl.BlockSpec((rb, DN_QK_W), lambda b, d, j: (rblock(b, d, j), 0)),
                  pl.BlockSpec((rb, DN_QK_W), lambda b, d, j: (rblock(b, d, j), 0)),
                  pl.BlockSpec((rb, DN_V_W), lambda b, d, j: (rblock(b, d, j), 0)),
                  pl.BlockSpec((1, rb, AUX_W), lambda b, d, j: (d, rblock(b, d, j), 0))],
        out_specs=pl.BlockSpec((1, rb, DN_V_W), lambda b, d, j: (d, rblock(b, d, j), 0)),
        out_shape=jax.ShapeDtypeStruct((2, t, DN_V_W), BF16),
        scratch_shapes=[pltpu.VMEM((DN_DK, DN_V_HEADS * DN_DV), F32),
                        pltpu.VMEM((cb, DN_V_HEADS, 2 * DN_CHUNK, DN_DK), BF16),
                        pltpu.VMEM((cb, ngroups, DN_GM, DN_DV), F32),
                        pltpu.VMEM((cb, ngroups, DN_GM, DN_DK), BF16),
                        pltpu.VMEM((cb, ngroups, DN_GM, DN_GM), BF16)],
        compiler_params=_cparams(("parallel", "parallel", "arbitrary")),
        name="delta_rule",
    )(q, k, v, aux)


def _attention_kernel(qt_ref, k_ref, vt_ref, o_ref, acc_ref, m_ref, l_ref, *, tk, seq):
    m_ref[...] = jnp.full(m_ref.shape, -jnp.inf, F32)
    l_ref[...] = jnp.zeros(l_ref.shape, F32)
    acc_ref[...] = jnp.zeros(acc_ref.shape, F32)

    def step(c, carry):
        ks = pl.ds(pl.multiple_of(c * tk, tk), tk)
        kc = k_ref[ks, :]
        vtc = vt_ref[:, ks]
        for g in range(ATT_GROUP):
            s = jnp.dot(kc, qt_ref[g * HEAD_DIM:(g + 1) * HEAD_DIM, :], preferred_element_type=F32)
            m_old = m_ref[g:g + 1, :]
            m_new = jnp.maximum(m_old, jnp.max(s, axis=0, keepdims=True))
            alpha = jnp.exp2(m_old - m_new)
            p = jnp.exp2(s - m_new)
            l_ref[g:g + 1, :] = alpha * l_ref[g:g + 1, :] + jnp.sum(p, axis=0, keepdims=True)
            m_ref[g:g + 1, :] = m_new
            acc_ref[g] = alpha * acc_ref[g] + jnp.dot(vtc, p.astype(BF16), preferred_element_type=F32)
        return carry

    nsteps = seq // tk
    lax.fori_loop(0, nsteps, step, 0, unroll=math.gcd(nsteps, ATT_UNROLL))
    for g in range(ATT_GROUP):
        out = acc_ref[g] / l_ref[g:g + 1, :]
        o_ref[:, g * HEAD_DIM:(g + 1) * HEAD_DIM] = jnp.transpose(out).astype(o_ref.dtype)


def _attention(aqt, ak, avt, *, batch, seq, tq, tk):
    t = ak.shape[0]
    nq = seq // tq
    gw = ATT_GROUP * HEAD_DIM
    return pl.pallas_call(
        functools.partial(_attention_kernel, tk=tk, seq=seq),
        grid=(batch, ATT_KV_HEADS, nq),
        in_specs=[pl.BlockSpec((gw, tq), lambda b, g, i: (g, b * nq + i)),
                  pl.BlockSpec((seq, HEAD_DIM), lambda b, g, i: (b, g)),
                  pl.BlockSpec((HEAD_DIM, seq), lambda b, g, i: (g, b))],
        out_specs=pl.BlockSpec((tq, gw), lambda b, g, i: (b * nq + i, g)),
        out_shape=jax.ShapeDtypeStruct((t, ATT_Q_W), BF16),
        scratch_shapes=[pltpu.VMEM((ATT_GROUP, HEAD_DIM, tq), F32), pltpu.VMEM((SUBLANES, tq), F32),
                        pltpu.VMEM((SUBLANES, tq), F32)],
        compiler_params=_cparams(("parallel", "parallel", "parallel")),
        name="attention",
    )(aqt, ak, avt)


def _merge_kernel(h_ref, o_ref, z_ref, att_ref, gd_ref, ga_ref, on_ref, wdn_ref, wat_ref, wo_ref, out_ref):
    o = o_ref[0].astype(F32) + o_ref[1].astype(F32)
    z = z_ref[...].astype(F32)
    parts = []
    for h in range(DN_V_HEADS):
        sl = slice(h * DN_DV, (h + 1) * DN_DV)
        parts.append((_rms(o[:, sl], on_ref[...]) * _silu(z[:, sl])).astype(BF16))
    o_dn = jnp.concatenate(parts, axis=1)
    y_dn = jnp.dot(o_dn, wdn_ref[...], preferred_element_type=F32)
    y_at = jnp.dot(att_ref[...], wat_ref[...], preferred_element_type=F32)
    merged = _sigmoid(gd_ref[...].astype(F32)) * y_dn + _sigmoid(ga_ref[...].astype(F32)) * y_at
    out_ref[...] = h_ref[...] + jnp.dot(merged.astype(BF16), wo_ref[...], preferred_element_type=F32)


def _merge(h, o, z, att, gd, ga, on, wdn, wat, wo, *, tm):
    t = h.shape[0]
    row = lambda w: pl.BlockSpec((tm, w), lambda i: (i, 0))
    return pl.pallas_call(
        _merge_kernel,
        grid=(t // tm,),
        in_specs=[row(D_MODEL), pl.BlockSpec((2, tm, DN_V_W), lambda i: (0, i, 0)), row(DN_V_W),
                  row(ATT_Q_W), row(D_MODEL), row(D_MODEL), _resident(on.shape),
                  _resident(wdn.shape), _resident(wat.shape), _resident(wo.shape)],
        out_specs=row(D_MODEL),
        out_shape=jax.ShapeDtypeStruct((t, D_MODEL), F32),
        compiler_params=_cparams(("parallel",)),
        name="merge",
    )(h, o, z, att, gd, ga, on, wdn, wat, wo)


def _mlp_ple_kernel(h_ref, pe_ref, nm_ref, np_ref, wup_ref, wdown_ref, wpg_ref, wple_ref, out_ref):
    h1 = h_ref[...]
    hm = jnp.dot(_rms(h1, nm_ref[...]).astype(BF16), wup_ref[...], preferred_element_type=F32)
    act = jnp.square(jnp.maximum(hm, 0.0)).astype(BF16)
    h2 = h1 + jnp.dot(act, wdown_ref[...], preferred_element_type=F32)
    gate = _sigmoid(jnp.dot(_rms(h2, np_ref[...]).astype(BF16), wpg_ref[...], preferred_element_type=F32))
    pp = jnp.dot(pe_ref[...].astype(BF16), wple_ref[...], preferred_element_type=F32)
    out_ref[...] = h2 + gate * pp


def _mlp_ple(h1, pe, nm, npl, wup, wdown, wpg, wple, *, tm):
    t = h1.shape[0]
    row = lambda w: pl.BlockSpec((tm, w), lambda i: (i, 0))
    return pl.pallas_call(
        _mlp_ple_kernel,
        grid=(t // tm,),
        in_specs=[row(D_MODEL), row(PLE_DIM), _resident(nm.shape), _resident(npl.shape),
                  _resident(wup.shape), _resident(wdown.shape), _resident(wpg.shape), _resident(wple.shape)],
        out_specs=row(D_MODEL),
        out_shape=jax.ShapeDtypeStruct((t, D_MODEL), F32),
        compiler_params=_cparams(("parallel",)),
        name="mlp_ple",
    )(h1, pe, nm, npl, wup, wdown, wpg, wple)


def _rope_tables(seq):
    pos = jnp.arange(seq)
    r = (pos // GRID_W).astype(F32)
    c = (pos % GRID_W).astype(F32)
    half = HEAD_DIM // 2
    inv = jnp.power(jnp.float32(ROPE_THETA), -jnp.arange(0, half, 2, dtype=F32) / half)
    ar = r[:, None] * inv
    ac = c[:, None] * inv
    cos = jnp.concatenate([jnp.cos(ar), jnp.cos(ar), jnp.cos(ac), jnp.cos(ac)], axis=1)
    sin = jnp.concatenate([-jnp.sin(ar), jnp.sin(ar), -jnp.sin(ac), jnp.sin(ac)], axis=1)
    return cos, sin


def _prepare_weights(norm_mix, w_in, dn_conv_w, dn_a_log, dn_dt_bias, dn_out_norm, w_dn_branch,
                     q_norm, k_norm, w_attn_branch, w_o, norm_mlp, w_up, w_down, norm_ple,
                     w_ple_gate, w_ple):
    bf = lambda w: w.astype(BF16)
    w_in = w_in[0]
    splits = (2 * DN_QK_W + DN_V_W, DN_V_W, 2 * DN_V_HEADS, 2 * DN_V_HEADS, ATT_Q_W, ATT_KV_W, ATT_KV_W,
              D_MODEL, D_MODEL)
    cols, start = [], 0
    for s in splits:
        cols.append(w_in[:, start:start + s])
        start += s
    w_qkv, w_z, w_b, w_a, w_aq, w_ak, w_av, w_gd, w_ga = cols
    pad = jnp.zeros((D_MODEL, LANES - 2 * DN_V_HEADS), F32)
    w_ba = jnp.concatenate([w_b, pad, w_a, pad], axis=1)
    in_ws = tuple(bf(w) for w in (w_qkv, w_z, w_ba, w_aq, w_ak, w_av, w_gd, w_ga))
    lane_pad = lambda x: jnp.pad(x.reshape(1, -1), ((0, 0), (0, LANES - x.size)))
    return dict(
        norm_mix=norm_mix[0].reshape(1, -1), in_ws=in_ws,
        conv=dn_conv_w[0],
        alog=lane_pad(dn_a_log[0]), dtb=lane_pad(dn_dt_bias[0]),
        out_norm=dn_out_norm[0].reshape(1, -1), w_dn=bf(w_dn_branch[0]),
        q_norm=q_norm[0].reshape(1, -1), k_norm=k_norm[0].reshape(1, -1),
        w_at=bf(w_attn_branch[0]), w_o=bf(w_o[0]),
        norm_mlp=norm_mlp[0].reshape(1, -1), w_up=bf(w_up[0]), w_down=bf(w_down[0]),
        norm_ple=norm_ple[0].reshape(1, -1), w_pg=bf(w_ple_gate[0]), w_ple=bf(w_ple[0]))


def _tiles(seq):
    return dict(tm_proj=min(256, seq), cb=min(8, seq // DN_CHUNK),
                tq=min(256, seq), tk=min(128, seq), tm_merge=min(256, seq), tm_mlp=min(256, seq))


def _encoder_layer(x, pe, p):
    batch, seq, _ = x.shape
    assert seq % DN_CHUNK == 0 and seq % GRID_W == 0
    t = batch * seq
    tl = _tiles(seq)
    h = x.reshape(t, D_MODEL)
    cos, sin = _rope_tables(seq)
    qn, kn, vc, aux, z, aq, ak, av, gd, ga = _in_proj(
        h, p["norm_mix"], cos, sin, p["q_norm"], p["k_norm"], p["in_ws"], p["conv"], p["alog"], p["dtb"],
        seq=seq, tm=tl["tm_proj"])
    o = _delta_rule(qn, kn, vc, aux, batch=batch, seq=seq, cb=tl["cb"])
    att = _attention(aq, ak, av, batch=batch, seq=seq, tq=tl["tq"], tk=tl["tk"])
    h1 = _merge(h, o, z, att, gd, ga, p["out_norm"], p["w_dn"], p["w_at"], p["w_o"], tm=tl["tm_merge"])
    h3 = _mlp_ple(h1, pe.reshape(t, PLE_DIM), p["norm_mlp"], p["norm_ple"], p["w_up"], p["w_down"],
                  p["w_pg"], p["w_ple"], tm=tl["tm_mlp"])
    return h3.reshape(batch, seq, D_MODEL)


@jax.jit
def kernel(x_prompt, x_sample, p_prompt, p_sample, norm_mix, w_in, dn_conv_w, dn_a_log, dn_dt_bias,
           dn_out_norm, w_dn_branch, q_norm, k_norm, w_attn_branch, w_o, norm_mlp, w_up, w_down,
           norm_ple, w_ple_gate, w_ple):
    assert norm_mix.shape[0] == 1, "single-layer trunk"
    p = _prepare_weights(norm_mix, w_in, dn_conv_w, dn_a_log, dn_dt_bias, dn_out_norm, w_dn_branch,
                         q_norm, k_norm, w_attn_branch, w_o, norm_mlp, w_up, w_down, norm_ple,
                         w_ple_gate, w_ple)
    y_prompt = _encoder_layer(x_prompt, p_prompt[0], p)
    y_sample = _encoder_layer(x_sample, p_sample[0], p)
    return (y_prompt, y_sample)
```

```python
import functools
import math

import jax
import jax.numpy as jnp
from jax import lax
from jax.experimental import pallas as pl
from jax.experimental.pallas import tpu as pltpu

F32 = jnp.float32
BF16 = jnp.bfloat16

D_MODEL = 1024
PLE_DIM = 256
D_FF = 4 * D_MODEL
NORM_EPS = 1e-6
GRID_W = 64
DN_QK_HEADS = 4
DN_V_HEADS = 8
DN_DK = 128
DN_DV = 128
DN_CHUNK = 64
DN_CONV = 5
DN_QK_W = DN_QK_HEADS * DN_DK
DN_V_W = DN_V_HEADS * DN_DV
ATT_HEADS = 8
ATT_KV_HEADS = 2
ATT_GROUP = ATT_HEADS // ATT_KV_HEADS
HEAD_DIM = 128
ROPE_THETA = 10000.0
LOG2_E = math.log2(math.e)
ATT_Q_W = ATT_HEADS * HEAD_DIM
ATT_KV_W = ATT_KV_HEADS * HEAD_DIM
ATT_UNROLL = 16

LANES = 128
SUBLANES = 8
VMEM_LIMIT_BYTES = 56 * 1024 * 1024

DN_GROUP = 4
DN_GM = DN_GROUP * DN_CHUNK
DN_PIPE = 2
AUX_W = LANES
AUX_GC, AUX_BETA, AUX_EGC, AUX_EGL, AUX_EG = 0, 1, 2, 3, 4


def _cparams(semantics):
    return pltpu.CompilerParams(dimension_semantics=semantics, vmem_limit_bytes=VMEM_LIMIT_BYTES)


def _resident(shape):
    nd = len(shape)
    return pl.BlockSpec(shape, lambda *_: (0,) * nd, pipeline_mode=pl.Buffered(1))


def _rms(x, gain):
    ms = jnp.mean(x * x, axis=-1, keepdims=True)
    return x * lax.rsqrt(ms + NORM_EPS) * gain


def _mm(a, b):
    return jnp.dot(a.astype(BF16), b.astype(BF16), preferred_element_type=F32)


def _mm_nt(a, b):
    return lax.dot_general(a.astype(BF16), b.astype(BF16), (((1,), (1,)), ((), ())),
                           preferred_element_type=F32)


def _mm_tn(a, b):
    return lax.dot_general(a.astype(BF16), b.astype(BF16), (((0,), (0,)), ((), ())),
                           preferred_element_type=F32)


def _sigmoid(x):
    return 1.0 / (1.0 + jnp.exp(-x))


def _silu(x):
    return x * _sigmoid(x)


def _swap32(x):
    lane = lax.broadcasted_iota(jnp.int32, x.shape, 1)
    first = (lane & 32) == 0
    return jnp.where(first, pltpu.roll(x, LANES - 32, 1), pltpu.roll(x, 32, 1))


def _norm_rope(x, gain, cos, sin, scale):
    y = _rms(x, gain)
    return (y * cos + _swap32(y) * sin) * scale


HALO = 16


def _conv_silu(ext_ref, w_ref, col0, width, tm):
    pad = DN_CONV // 2
    acc = None
    for tap in range(DN_CONV):
        start = HALO - pad + tap
        term = ext_ref[start:start + tm, col0:col0 + width] * w_ref[tap:tap + 1, col0:col0 + width]
        acc = term if acc is None else acc + term
    return _silu(acc)


def _l2norm_heads(x, scale):
    outs = []
    for h in range(DN_QK_HEADS):
        xh = x[:, h * DN_DK:(h + 1) * DN_DK]
        r = lax.rsqrt(jnp.sum(xh * xh, axis=-1, keepdims=True) + NORM_EPS)
        outs.append(xh * (r * scale))
    return jnp.concatenate(outs, axis=1)


def _split3(x):
    hi = x.astype(BF16)
    r1 = x - hi.astype(F32)
    mid = r1.astype(BF16)
    lo = (r1 - mid.astype(F32)).astype(BF16)
    return hi, mid, lo


def _ones_mm(mask_bf, x):
    hi, mid, lo = _split3(x)
    dot = lambda p: jnp.dot(mask_bf, p, preferred_element_type=F32)
    return dot(hi) + dot(mid) + dot(lo)


def _place(x, src, dst):
    shift = (dst - src) % LANES
    return x if shift == 0 else pltpu.roll(x, shift, 1)


def _gate_fields(ba, alog, dtb, aux_o, tm):
    beta = _sigmoid(ba[:, 0:LANES])
    a = ba[:, LANES:2 * LANES] + dtb
    softplus = jnp.maximum(a, 0.0) + jnp.log(1.0 + jnp.exp(-jnp.abs(a)))
    g = -jnp.exp(alog) * softplus

    ri = lax.broadcasted_iota(jnp.int32, (tm, tm), 0)
    ci = lax.broadcasted_iota(jnp.int32, (tm, tm), 1)
    same = lax.shift_right_logical(ri, 6) == lax.shift_right_logical(ci, 6)
    one = lambda m: jnp.where(m, 1.0, 0.0).astype(BF16)
    pre = _ones_mm(one(same & (ci <= ri)), g)
    suf = _ones_mm(one(same & (ci >= ri)), g)
    tot = _ones_mm(one(same), g)
    lane = lax.broadcasted_iota(jnp.int32, (tm, LANES), 1)
    gc = jnp.where(lane < DN_V_HEADS, pre, suf)
    egc = jnp.exp(gc)
    egl = jnp.exp(tot - gc)
    eg = jnp.exp(tot)
    for d in range(2):
        src = d * DN_V_HEADS
        aux = jnp.zeros((tm, AUX_W), F32)
        for field, val in ((AUX_GC, gc), (AUX_BETA, beta), (AUX_EGC, egc), (AUX_EGL, egl), (AUX_EG, eg)):
            dst = field * DN_V_HEADS
            aux = jnp.where((lane >= dst) & (lane < dst + DN_V_HEADS), _place(val, src, dst), aux)
        aux_o[d] = aux


def _in_proj_kernel(x_ref, xp_ref, xn_ref, gain_ref, cos_ref, sin_ref, qn_ref, kn_ref,
                    wqkv_ref, wz_ref, wba_ref, waq_ref, wak_ref, wav_ref, wgd_ref, wga_ref,
                    conv_ref, alog_ref, dtb_ref,
                    q_o, k_o, v_o, aux_o, z_o, aq_o, ak_o, av_o, gd_o, ga_o, ext_ref, *, tm, nseq):
    i = pl.program_id(0)
    first = (i % nseq) == 0
    last = (i % nseq) == nseq - 1
    gain = gain_ref[...]
    norm = lambda ref: _rms(ref[...], gain).astype(BF16)
    u = norm(x_ref)

    def proj(w_ref):
        return jnp.dot(u, w_ref[...], preferred_element_type=F32)

    u_ext = jnp.concatenate([norm(xp_ref), u, norm(xn_ref)], axis=0)
    ext_ref[...] = jnp.dot(u_ext, wqkv_ref[...], preferred_element_type=F32)
    ext_ref[0:HALO, :] = jnp.where(first, 0.0, ext_ref[0:HALO, :])
    ext_ref[HALO + tm:, :] = jnp.where(last, 0.0, ext_ref[HALO + tm:, :])
    q = _conv_silu(ext_ref, conv_ref, 0, DN_QK_W, tm)
    q_o[...] = _l2norm_heads(q, DN_DK ** -0.5).astype(q_o.dtype)
    k = _conv_silu(ext_ref, conv_ref, DN_QK_W, DN_QK_W, tm)
    k_o[...] = _l2norm_heads(k, 1.0).astype(k_o.dtype)
    v_o[...] = _conv_silu(ext_ref, conv_ref, 2 * DN_QK_W, DN_V_W, tm).astype(v_o.dtype)
    _gate_fields(proj(wba_ref), alog_ref[...], dtb_ref[...], aux_o, tm)

    z_o[...] = proj(wz_ref).astype(z_o.dtype)
    gd_o[...] = proj(wgd_ref).astype(gd_o.dtype)
    ga_o[...] = proj(wga_ref).astype(ga_o.dtype)

    cos = cos_ref[...]
    sin = sin_ref[...]
    aq = proj(waq_ref)
    for h in range(ATT_HEADS):
        sl = slice(h * HEAD_DIM, (h + 1) * HEAD_DIM)
        qh = _norm_rope(aq[:, sl], qn_ref[...], cos, sin, HEAD_DIM ** -0.5 * LOG2_E)
        aq_o[sl, :] = jnp.transpose(qh).astype(aq_o.dtype)
    ak = proj(wak_ref)
    av = proj(wav_ref)
    for h in range(ATT_KV_HEADS):
        sl = slice(h * HEAD_DIM, (h + 1) * HEAD_DIM)
        ak_o[:, sl] = _norm_rope(ak[:, sl], kn_ref[...], cos, sin, 1.0).astype(ak_o.dtype)
        av_o[sl, :] = jnp.transpose(av[:, sl]).astype(av_o.dtype)


def _in_proj(x, gain, cos, sin, qn, kn, ws, conv, alog, dtb, *, seq, tm):
    t = x.shape[0]
    nseq = seq // tm
    hb = tm // HALO
    nhb = t // HALO
    row = lambda w: pl.BlockSpec((tm, w), lambda i: (i, 0))
    tab = pl.BlockSpec((tm, HEAD_DIM), lambda i: (i % nseq, 0))
    col = lambda w: pl.BlockSpec((w, tm), lambda i: (0, i))
    sds = jax.ShapeDtypeStruct
    outs = [(row(DN_QK_W), sds((t, DN_QK_W), BF16)), (row(DN_QK_W), sds((t, DN_QK_W), BF16)),
            (row(DN_V_W), sds((t, DN_V_W), BF16)),
            (pl.BlockSpec((2, tm, AUX_W), lambda i: (0, i, 0)), sds((2, t, AUX_W), F32)),
            (row(DN_V_W), sds((t, DN_V_W), BF16)), (col(ATT_Q_W), sds((ATT_Q_W, t), BF16)),
            (row(ATT_KV_W), sds((t, ATT_KV_W), BF16)), (col(ATT_KV_W), sds((ATT_KV_W, t), BF16)),
            (row(D_MODEL), sds((t, D_MODEL), BF16)), (row(D_MODEL), sds((t, D_MODEL), BF16))]
    return pl.pallas_call(
        functools.partial(_in_proj_kernel, tm=tm, nseq=nseq),
        grid=(t // tm,),
        in_specs=[row(D_MODEL),
                  pl.BlockSpec((HALO, D_MODEL), lambda i: (jnp.maximum(i * hb - 1, 0), 0)),
                  pl.BlockSpec((HALO, D_MODEL), lambda i: (jnp.minimum((i + 1) * hb, nhb - 1), 0)),
                  _resident((1, D_MODEL)), tab, tab, _resident((1, HEAD_DIM)), _resident((1, HEAD_DIM))]
        + [_resident(w.shape) for w in ws]
        + [_resident(conv.shape), _resident(alog.shape), _resident(dtb.shape)],
        out_specs=[o[0] for o in outs],
        out_shape=[o[1] for o in outs],
        scratch_shapes=[pltpu.VMEM((tm + 2 * HALO, 2 * DN_QK_W + DN_V_W), F32)],
        compiler_params=_cparams(("parallel",)),
        name="in_proj",
    )(x, x, x, gain, cos, sin, qn, kn, *ws, conv, alog, dtb)


def _unit_lower_inverses(mats, eye, m16, m32, expand):
    each = lambda f, *ls: [f(*xs) for xs in zip(*ls)]
    a32 = [jnp.where(m32 & ~m16, a, 0.0).astype(BF16) for a in mats]
    a64 = [jnp.where(m32, 0.0, a).astype(BF16) for a in mats]
    n = [jnp.where(m16, -a, 0.0) for a in mats]
    x = [eye + ni for ni in n]
    p = each(lambda ni: _mm(ni, expand(ni)), n)
    yield
    for level in range(3):
        pe = [expand(pi) for pi in p]
        x = each(lambda xi, pei: xi + _mm(xi, pei), x, pe)
        if level < 2:
            p = each(_mm, p, pe)
        yield
    for off in (a32, a64):
        t = each(lambda oi, xi: _mm(oi, expand(xi)), off, x)
        yield
        x = each(lambda xi, ti: xi - _mm(xi, expand(ti)), x, t)
        yield
    return x


def _interleave(*gens):
    live = list(gens)
    while live:
        for g in list(live):
            try:
                next(g)
            except StopIteration:
                live.remove(g)


def _delta_rule_kernel(q_ref, k_ref, v_ref, aux_ref, o_ref, s_ref, wq_ref, u_ref, kd_ref, aqk_ref, *, cb):
    d = pl.program_id(1)
    j = pl.program_id(2)

    @pl.when(j == 0)
    def _():
        s_ref[...] = jnp.zeros_like(s_ref)

    gm = DN_GM
    blk = lambda x, s: lax.shift_right_logical(x, s)
    bd = (blk(lax.broadcasted_iota(jnp.int32, (gm, gm), 0), 6)
          == blk(lax.broadcasted_iota(jnp.int32, (gm, gm), 1), 6))
    ri = lax.broadcasted_iota(jnp.int32, (DN_CHUNK, gm), 0)
    lane = lax.broadcasted_iota(jnp.int32, (DN_CHUNK, gm), 1)
    ci = lane & (DN_CHUNK - 1)
    lblock = blk(lane, 6)
    rel = (ri - ci) * (1 - 2 * d)
    incl = rel >= 0
    strict = rel > 0
    m16 = blk(ri, 4) == blk(ci, 4)
    m32 = blk(ri, 5) == blk(ci, 5)
    eye = jnp.where(ri == ci, 1.0, 0.0).astype(F32)

    def expand(c):
        return jnp.where(bd, jnp.concatenate([c.astype(F32)] * DN_GROUP, axis=0), 0.0).astype(BF16)

    ngroups = DN_V_HEADS // DN_GROUP

    def chunk_rows(jj):
        c = jnp.where(d == 0, jj, cb - 1 - jj)
        return pl.ds(pl.multiple_of(c * DN_CHUNK, DN_CHUNK), DN_CHUNK)

    f32 = lambda x: x.astype(F32)

    def local_work(jjs):
        chains = [(jj, hg) for jj in jjs for hg in range(ngroups)]
        amats, rhss = [], []
        for jj, hg in chains:
            rows = chunk_rows(jj)
            aux = aux_ref[0, rows, :]
            h0 = hg * DN_GROUP

            def cols(field):
                base = field * DN_V_HEADS + h0
                return [aux[:, base + r:base + r + 1] for r in range(DN_GROUP)]

            def stacked(field):
                return jnp.concatenate(cols(field), axis=0)

            def blocked(field):
                c = cols(field)
                return jnp.where(lblock < 2, jnp.where(lblock == 0, c[0], c[1]),
                                 jnp.where(lblock == 2, c[2], c[3]))

            gc_col, beta_col, egc_col, egl_col = (stacked(f) for f in (AUX_GC, AUX_BETA, AUX_EGC, AUX_EGL))
            gc_row = jnp.transpose(jnp.broadcast_to(gc_col, (gm, LANES)))[0:1, :]

            qk_heads = [(h0 + r) // (DN_V_HEADS // DN_QK_HEADS) for r in range(DN_GROUP)]
            khead = lambda ref, h: ref[rows, h * DN_DK:(h + 1) * DN_DK]
            ks = jnp.concatenate([khead(k_ref, h) for h in qk_heads], axis=0)
            qs = jnp.concatenate([khead(q_ref, h) for h in qk_heads], axis=0)
            vs = jnp.concatenate([v_ref[rows, (h0 + r) * DN_DV:(h0 + r + 1) * DN_DV]
                                  for r in range(DN_GROUP)], axis=0)

            first = lblock < 2
            gram = jnp.where(first, _mm_nt(khead(k_ref, qk_heads[0]), ks), _mm_nt(khead(k_ref, qk_heads[2]), ks))
            qk = jnp.where(first, _mm_nt(khead(q_ref, qk_heads[0]), ks), _mm_nt(khead(q_ref, qk_heads[2]), ks))
            decay = jnp.where(incl, jnp.exp(jnp.where(incl, blocked(AUX_GC) - gc_row, 0.0)), 0.0)
            amats.append(jnp.where(strict, blocked(AUX_BETA) * gram * decay, 0.0))
            aqk_ref[jj, hg] = expand(qk * decay)
            rhss.append(jnp.concatenate([f32(vs) * beta_col, f32(ks) * (beta_col * egc_col)],
                                        axis=1).astype(BF16))
            kd_ref[jj, hg] = (f32(ks) * egl_col).astype(BF16)
            qd = (f32(qs) * egc_col).astype(BF16)
            for r in range(DN_GROUP):
                wq_ref[jj, h0 + r, DN_CHUNK:2 * DN_CHUNK, :] = qd[r * DN_CHUNK:(r + 1) * DN_CHUNK]
        yield
        tinvs = yield from _unit_lower_inverses(amats, eye, m16, m32, expand)
        for (jj, hg), tinv, rhs in zip(chains, tinvs, rhss):
            sol = jnp.dot(expand(tinv), rhs, preferred_element_type=F32)
            u_ref[jj, hg] = sol[:, 0:DN_DV]
            w = sol[:, DN_DV:DN_DV + DN_DK].astype(BF16)
            for r in range(DN_GROUP):
                wq_ref[jj, hg * DN_GROUP + r, 0:DN_CHUNK, :] = w[r * DN_CHUNK:(r + 1) * DN_CHUNK]
        yield

    def recurrence(jjs):
        hslice = lambda h: slice(h * DN_DV, (h + 1) * DN_DV)
        for jj in jjs:
            rows = chunk_rows(jj)
            eg_row = aux_ref[0, pl.ds(rows.start, 1), :]
            states = [s_ref[:, hslice(h)] for h in range(DN_V_HEADS)]
            res = [jnp.dot(wq_ref[jj, h], states[h].astype(BF16), preferred_element_type=F32)
                   for h in range(DN_V_HEADS)]
            yield
            vnew = []
            for h in range(DN_V_HEADS):
                hg, r = divmod(h, DN_GROUP)
                vnew.append(u_ref[jj, hg, r * DN_CHUNK:(r + 1) * DN_CHUNK, :] - res[h][0:DN_CHUNK])
            for h in range(DN_V_HEADS):
                hg, r = divmod(h, DN_GROUP)
                eg = eg_row[:, AUX_EG * DN_V_HEADS + h:AUX_EG * DN_V_HEADS + h + 1]
                kd = kd_ref[jj, hg, r * DN_CHUNK:(r + 1) * DN_CHUNK, :]
                s_ref[:, hslice(h)] = states[h] * eg + _mm_tn(kd, vnew[h])
            yield
            for hg in range(ngroups):
                hs = range(hg * DN_GROUP, (hg + 1) * DN_GROUP)
                out = (jnp.concatenate([res[h][DN_CHUNK:2 * DN_CHUNK] for h in hs], axis=0)
                       + _mm(aqk_ref[jj, hg], jnp.concatenate([vnew[h] for h in hs], axis=0)))
                for r, h in enumerate(hs):
                    o_ref[0, rows, hslice(h)] = out[r * DN_CHUNK:(r + 1) * DN_CHUNK].astype(o_ref.dtype)
            yield

    stages = [range(i, min(i + DN_PIPE, cb)) for i in range(0, cb, DN_PIPE)]
    _interleave(local_work(stages[0]))
    for prev, cur in zip(stages, stages[1:]):
        _interleave(local_work(cur), recurrence(prev))
    _interleave(recurrence(stages[-1]))


def _delta_rule(q, k, v, aux, *, batch, seq, cb):
    t = q.shape[0]
    rb = cb * DN_CHUNK
    nb = seq // rb
    ngroups = DN_V_HEADS // DN_GROUP

    def rblock(b, d, j):
        return b * nb + jnp.where(d == 0, j, nb - 1 - j)

    return pl.pallas_call(
        functools.partial(_delta_rule_kernel, cb=cb),
        grid=(batch, 2, nb),
        in_specs=[pl.BlockSpec((rb, DN_QK_W), lambda b, d, j: (rblock(b, d, j), 0)),
                  pl.BlockSpec((rb, DN_QK_W), lambda b, d, j: (rblock(b, d, j), 0)),
                  pl.BlockSpec((rb, DN_V_W), lambda b, d, j: (rblock(b, d, j), 0)),
                  pl.BlockSpec((1, rb, AUX_W), lambda b, d, j: (d, rblock(b, d, j), 0))],
        out_specs=pl.BlockSpec((1, rb, DN_V_W), lambda b, d, j: (d, rblock(b, d, j), 0)),
        out_shape=jax.ShapeDtypeStruct((2, t, DN_V_W), BF16),
        scratch_shapes=[pltpu.VMEM((DN_DK, DN_V_HEADS * DN_DV), F32),
                        pltpu.VMEM((cb, DN_V_HEADS, 2 * DN_CHUNK, DN_DK), BF16),
                        pltpu.VMEM((cb, ngroups, DN_GM, DN_DV), F32),
                        pltpu.VMEM((cb, ngroups, DN_GM, DN_DK), BF16),
                        pltpu.VMEM((cb, ngroups, DN_GM, DN_GM), BF16)],
        compiler_params=_cparams(("parallel", "parallel", "arbitrary")),
        name="delta_rule",
    )(q, k, v, aux)


def _attention_kernel(qt_ref, k_ref, vt_ref, o_ref, acc_ref, m_ref, l_ref, *, tk, seq):
    m_ref[...] = jnp.full(m_ref.shape, -jnp.inf, F32)
    l_ref[...] = jnp.zeros(l_ref.shape, F32)
    acc_ref[...] = jnp.zeros(acc_ref.shape, F32)

    def step(c, carry):
        ks = pl.ds(pl.multiple_of(c * tk, tk), tk)
        kc = k_ref[ks, :]
        vtc = vt_ref[:, ks]
        for g in range(ATT_GROUP):
            s = jnp.dot(kc, qt_ref[g * HEAD_DIM:(g + 1) * HEAD_DIM, :], preferred_element_type=F32)
            m_old = m_ref[g:g + 1, :]
            m_new = jnp.maximum(m_old, jnp.max(s, axis=0, keepdims=True))
            alpha = jnp.exp2(m_old - m_new)
            p = jnp.exp2(s - m_new)
            l_ref[g:g + 1, :] = alpha * l_ref[g:g + 1, :] + jnp.sum(p, axis=0, keepdims=True)
            m_ref[g:g + 1, :] = m_new
            acc_ref[g] = alpha * acc_ref[g] + jnp.dot(vtc, p.astype(BF16), preferred_element_type=F32)
        return carry

    nsteps = seq // tk
    lax.fori_loop(0, nsteps, step, 0, unroll=math.gcd(nsteps, ATT_UNROLL))
    for g in range(ATT_GROUP):
        out = acc_ref[g] / l_ref[g:g + 1, :]
        o_ref[:, g * HEAD_DIM:(g + 1) * HEAD_DIM] = jnp.transpose(out).astype(o_ref.dtype)


def _attention(aqt, ak, avt, *, batch, seq, tq, tk):
    t = ak.shape[0]
    nq = seq // tq
    gw = ATT_GROUP * HEAD_DIM
    return pl.pallas_call(
        functools.partial(_attention_kernel, tk=tk, seq=seq),
        grid=(batch, ATT_KV_HEADS, nq),
        in_specs=[pl.BlockSpec((gw, tq), lambda b, g, i: (g, b * nq + i)),
                  pl.BlockSpec((seq, HEAD_DIM), lambda b, g, i: (b, g)),
                  pl.BlockSpec((HEAD_DIM, seq), lambda b, g, i: (g, b))],
        out_specs=pl.BlockSpec((tq, gw), lambda b, g, i: (b * nq + i, g)),
        out_shape=jax.ShapeDtypeStruct((t, ATT_Q_W), BF16),
        scratch_shapes=[pltpu.VMEM((ATT_GROUP, HEAD_DIM, tq), F32), pltpu.VMEM((SUBLANES, tq), F32),
                        pltpu.VMEM((SUBLANES, tq), F32)],
        compiler_params=_cparams(("parallel", "parallel", "parallel")),
        name="attention",
    )(aqt, ak, avt)


def _merge_mlp_kernel(h_ref, o_ref, z_ref, att_ref, gd_ref, ga_ref, pe_ref, on_ref, nm_ref, np_ref,
                      wdn_ref, wat_ref, wo_ref, wup_ref, wdown_ref, wpg_ref, wple_ref, out_ref):
    dot = lambda a, w_ref: jnp.dot(a, w_ref[...], preferred_element_type=F32)
    o = o_ref[0].astype(F32) + o_ref[1].astype(F32)
    z = z_ref[...].astype(F32)
    parts = []
    for h in range(DN_V_HEADS):
        sl = slice(h * DN_DV, (h + 1) * DN_DV)
        parts.append((_rms(o[:, sl], on_ref[...]) * _silu(z[:, sl])).astype(BF16))
    y_dn = dot(jnp.concatenate(parts, axis=1), wdn_ref)
    y_at = dot(att_ref[...], wat_ref)
    merged = _sigmoid(gd_ref[...].astype(F32)) * y_dn + _sigmoid(ga_ref[...].astype(F32)) * y_at
    h1 = h_ref[...] + dot(merged.astype(BF16), wo_ref)
    hm = dot(_rms(h1, nm_ref[...]).astype(BF16), wup_ref)
    h2 = h1 + dot(jnp.square(jnp.maximum(hm, 0.0)).astype(BF16), wdown_ref)
    gate = _sigmoid(dot(_rms(h2, np_ref[...]).astype(BF16), wpg_ref))
    out_ref[...] = h2 + gate * dot(pe_ref[...].astype(BF16), wple_ref)


def _merge_mlp(h, o, z, att, gd, ga, pe, on, nm, npl, wdn, wat, wo, wup, wdown, wpg, wple, *, tm):
    t = h.shape[0]
    row = lambda w: pl.BlockSpec((tm, w), lambda i: (i, 0))
    consts = (on, nm, npl, wdn, wat, wo, wup, wdown, wpg, wple)
    return pl.pallas_call(
        _merge_mlp_kernel,
        grid=(t // tm,),
        in_specs=[row(D_MODEL), pl.BlockSpec((2, tm, DN_V_W), lambda i: (0, i, 0)), row(DN_V_W),
                  row(ATT_Q_W), row(D_MODEL), row(D_MODEL), row(PLE_DIM)]
        + [_resident(c.shape) for c in consts],
        out_specs=row(D_MODEL),
        out_shape=jax.ShapeDtypeStruct((t, D_MODEL), F32),
        compiler_params=_cparams(("parallel",)),
        name="merge_mlp",
    )(h, o, z, att, gd, ga, pe, *consts)


def _rope_tables(seq):
    pos = jnp.arange(seq)
    r = (pos // GRID_W).astype(F32)
    c = (pos % GRID_W).astype(F32)
    half = HEAD_DIM // 2
    inv = jnp.power(jnp.float32(ROPE_THETA), -jnp.arange(0, half, 2, dtype=F32) / half)
    ar = r[:, None] * inv
    ac = c[:, None] * inv
    cos = jnp.concatenate([jnp.cos(ar), jnp.cos(ar), jnp.cos(ac), jnp.cos(ac)], axis=1)
    sin = jnp.concatenate([-jnp.sin(ar), jnp.sin(ar), -jnp.sin(ac), jnp.sin(ac)], axis=1)
    return cos, sin


def _prepare_weights(norm_mix, w_in, dn_conv_w, dn_a_log, dn_dt_bias, dn_out_norm, w_dn_branch,
                     q_norm, k_norm, w_attn_branch, w_o, norm_mlp, w_up, w_down, norm_ple,
                     w_ple_gate, w_ple):
    bf = lambda w: w.astype(BF16)
    w_in = w_in[0]
    splits = (2 * DN_QK_W + DN_V_W, DN_V_W, 2 * DN_V_HEADS, 2 * DN_V_HEADS, ATT_Q_W, ATT_KV_W, ATT_KV_W,
              D_MODEL, D_MODEL)
    cols, start = [], 0
    for s in splits:
        cols.append(w_in[:, start:start + s])
        start += s
    w_qkv, w_z, w_b, w_a, w_aq, w_ak, w_av, w_gd, w_ga = cols
    pad = jnp.zeros((D_MODEL, LANES - 2 * DN_V_HEADS), F32)
    w_ba = jnp.concatenate([w_b, pad, w_a, pad], axis=1)
    in_ws = tuple(bf(w) for w in (w_qkv, w_z, w_ba, w_aq, w_ak, w_av, w_gd, w_ga))
    lane_pad = lambda x: jnp.pad(x.reshape(1, -1), ((0, 0), (0, LANES - x.size)))
    return dict(
        norm_mix=norm_mix[0].reshape(1, -1), in_ws=in_ws,
        conv=dn_conv_w[0],
        alog=lane_pad(dn_a_log[0]), dtb=lane_pad(dn_dt_bias[0]),
        out_norm=dn_out_norm[0].reshape(1, -1), w_dn=bf(w_dn_branch[0]),
        q_norm=q_norm[0].reshape(1, -1), k_norm=k_norm[0].reshape(1, -1),
        w_at=bf(w_attn_branch[0]), w_o=bf(w_o[0]),
        norm_mlp=norm_mlp[0].reshape(1, -1), w_up=bf(w_up[0]), w_down=bf(w_down[0]),
        norm_ple=norm_ple[0].reshape(1, -1), w_pg=bf(w_ple_gate[0]), w_ple=bf(w_ple[0]))


def _tiles(seq):
    return dict(tm_proj=min(256, seq), cb=min(8, seq // DN_CHUNK),
                tq=min(256, seq), tk=min(128, seq), tm_merge=min(256, seq))


def _encoder_layer(x, pe, p):
    batch, seq, _ = x.shape
    assert seq % DN_CHUNK == 0 and seq % GRID_W == 0
    t = batch * seq
    tl = _tiles(seq)
    h = x.reshape(t, D_MODEL)
    cos, sin = _rope_tables(seq)
    qn, kn, vc, aux, z, aq, ak, av, gd, ga = _in_proj(
        h, p["norm_mix"], cos, sin, p["q_norm"], p["k_norm"], p["in_ws"], p["conv"], p["alog"], p["dtb"],
        seq=seq, tm=tl["tm_proj"])
    o = _delta_rule(qn, kn, vc, aux, batch=batch, seq=seq, cb=tl["cb"])
    att = _attention(aq, ak, av, batch=batch, seq=seq, tq=tl["tq"], tk=tl["tk"])
    h3 = _merge_mlp(h, o, z, att, gd, ga, pe.reshape(t, PLE_DIM), p["out_norm"], p["norm_mlp"], p["norm_ple"],
                    p["w_dn"], p["w_at"], p["w_o"], p["w_up"], p["w_down"], p["w_pg"], p["w_ple"],
                    tm=tl["tm_merge"])
    return h3.reshape(batch, seq, D_MODEL)


@jax.jit
def kernel(x_prompt, x_sample, p_prompt, p_sample, norm_mix, w_in, dn_conv_w, dn_a_log, dn_dt_bias,
           dn_out_norm, w_dn_branch, q_norm, k_norm, w_attn_branch, w_o, norm_mlp, w_up, w_down,
           norm_ple, w_ple_gate, w_ple):
    assert norm_mix.shape[0] == 1, "single-layer trunk"
    p = _prepare_weights(norm_mix, w_in, dn_conv_w, dn_a_log, dn_dt_bias, dn_out_norm, w_dn_branch,
                         q_norm, k_norm, w_attn_branch, w_o, norm_mlp, w_up, w_down, norm_ple,
                         w_ple_gate, w_ple)
    y_prompt = _encoder_layer(x_prompt, p_prompt[0], p)
    y_sample = _encoder_layer(x_sample, p_sample[0], p)
    return (y_prompt, y_sample)
```

```python
import functools
import math

import jax
import jax.numpy as jnp
from jax import lax
from jax.experimental import pallas as pl
from jax.experimental.pallas import tpu as pltpu

F32 = jnp.float32
BF16 = jnp.bfloat16

D_MODEL = 1024
PLE_DIM = 256
D_FF = 4 * D_MODEL
NORM_EPS = 1e-6
GRID_W = 64
DN_QK_HEADS = 4
DN_V_HEADS = 8
DN_DK = 128
DN_DV = 128
DN_CHUNK = 64
DN_CONV = 5
DN_QK_W = DN_QK_HEADS * DN_DK
DN_V_W = DN_V_HEADS * DN_DV
ATT_HEADS = 8
ATT_KV_HEADS = 2
ATT_GROUP = ATT_HEADS // ATT_KV_HEADS
HEAD_DIM = 128
ROPE_THETA = 10000.0
LOG2_E = math.log2(math.e)
ATT_Q_W = ATT_HEADS * HEAD_DIM
ATT_KV_W = ATT_KV_HEADS * HEAD_DIM
ATT_UNROLL = 32

LANES = 128
SUBLANES = 8
VMEM_LIMIT_BYTES = 56 * 1024 * 1024

DN_GROUP = 4
DN_GM = DN_GROUP * DN_CHUNK
DN_PIPE = 2
AUX_W = LANES
AUX_GC, AUX_BETA, AUX_TOT, AUX_EG = 0, 1, 2, 3


def _cparams(semantics):
    return pltpu.CompilerParams(dimension_semantics=semantics, vmem_limit_bytes=VMEM_LIMIT_BYTES)


def _resident(shape):
    nd = len(shape)
    return pl.BlockSpec(shape, lambda *_: (0,) * nd, pipeline_mode=pl.Buffered(1))


def _rms(x, gain):
    ms = jnp.mean(x * x, axis=-1, keepdims=True)
    return x * lax.rsqrt(ms + NORM_EPS) * gain


def _mm(a, b):
    return jnp.dot(a.astype(BF16), b.astype(BF16), preferred_element_type=F32)


def _mm_nt(a, b):
    return lax.dot_general(a.astype(BF16), b.astype(BF16), (((1,), (1,)), ((), ())),
                           preferred_element_type=F32)


def _mm_tn(a, b):
    return lax.dot_general(a.astype(BF16), b.astype(BF16), (((0,), (0,)), ((), ())),
                           preferred_element_type=F32)


def _sigmoid(x):
    return 1.0 / (1.0 + jnp.exp(-x))


def _silu(x):
    return x * _sigmoid(x)


def _swap32(x):
    lane = lax.broadcasted_iota(jnp.int32, x.shape, 1)
    first = (lane & 32) == 0
    return jnp.where(first, pltpu.roll(x, LANES - 32, 1), pltpu.roll(x, 32, 1))


def _norm_rope(x, gain, cos, sin, scale):
    y = _rms(x, gain)
    return (y * cos + _swap32(y) * sin) * scale


HALO = 16


def _conv_silu(ext_ref, w_ref, col0, width, tm):
    pad = DN_CONV // 2
    acc = None
    for tap in range(DN_CONV):
        start = HALO - pad + tap
        term = ext_ref[start:start + tm, col0:col0 + width] * w_ref[tap:tap + 1, col0:col0 + width]
        acc = term if acc is None else acc + term
    return _silu(acc)


def _l2norm_heads(x, scale):
    outs = []
    for h in range(DN_QK_HEADS):
        xh = x[:, h * DN_DK:(h + 1) * DN_DK]
        r = lax.rsqrt(jnp.sum(xh * xh, axis=-1, keepdims=True) + NORM_EPS)
        outs.append(xh * (r * scale))
    return jnp.concatenate(outs, axis=1)


def _split3(x):
    hi = x.astype(BF16)
    r1 = x - hi.astype(F32)
    mid = r1.astype(BF16)
    lo = (r1 - mid.astype(F32)).astype(BF16)
    return hi, mid, lo


def _ones_mm(mask_bf, x):
    hi, mid, lo = _split3(x)
    dot = lambda p: jnp.dot(mask_bf, p, preferred_element_type=F32)
    return dot(hi) + dot(mid) + dot(lo)


def _place(x, src, dst):
    shift = (dst - src) % LANES
    return x if shift == 0 else pltpu.roll(x, shift, 1)


def _gate_fields(ba, alog, dtb, aux_o, tm):
    beta = _sigmoid(ba[:, 0:LANES])
    a = ba[:, LANES:2 * LANES] + dtb
    softplus = jnp.maximum(a, 0.0) + jnp.log(1.0 + jnp.exp(-jnp.abs(a)))
    g = -jnp.exp(alog) * softplus

    ri = lax.broadcasted_iota(jnp.int32, (tm, tm), 0)
    ci = lax.broadcasted_iota(jnp.int32, (tm, tm), 1)
    same = lax.shift_right_logical(ri, 6) == lax.shift_right_logical(ci, 6)
    one = lambda m: jnp.where(m, 1.0, 0.0).astype(BF16)
    pre = _ones_mm(one(same & (ci <= ri)), g)
    suf = _ones_mm(one(same & (ci >= ri)), g)
    tot = _ones_mm(one(same), g)
    lane = lax.broadcasted_iota(jnp.int32, (tm, LANES), 1)
    gc = jnp.where(lane < DN_V_HEADS, pre, suf)
    eg = jnp.exp(tot)
    for d in range(2):
        src = d * DN_V_HEADS
        aux = jnp.zeros((tm, AUX_W), F32)
        for field, val in ((AUX_GC, gc), (AUX_BETA, beta), (AUX_TOT, tot), (AUX_EG, eg)):
            dst = field * DN_V_HEADS
            aux = jnp.where((lane >= dst) & (lane < dst + DN_V_HEADS), _place(val, src, dst), aux)
        aux_o[d] = aux


def _in_proj_kernel(x_ref, xp_ref, xn_ref, gain_ref, cos_ref, sin_ref, qn_ref, kn_ref,
                    wqkv_ref, wz_ref, wba_ref, waq_ref, wak_ref, wav_ref, wgd_ref, wga_ref,
                    conv_ref, alog_ref, dtb_ref,
                    q_o, k_o, v_o, aux_o, z_o, aq_o, ak_o, av_o, gd_o, ga_o, ext_ref, *, tm, nseq):
    i = pl.program_id(0)
    first = (i % nseq) == 0
    last = (i % nseq) == nseq - 1
    gain = gain_ref[...]
    norm = lambda ref: _rms(ref[...], gain).astype(BF16)
    u = norm(x_ref)

    def proj(w_ref):
        return jnp.dot(u, w_ref[...], preferred_element_type=F32)

    u_ext = jnp.concatenate([norm(xp_ref), u, norm(xn_ref)], axis=0)
    ext_ref[...] = jnp.dot(u_ext, wqkv_ref[...], preferred_element_type=F32)
    ext_ref[0:HALO, :] = jnp.where(first, 0.0, ext_ref[0:HALO, :])
    ext_ref[HALO + tm:, :] = jnp.where(last, 0.0, ext_ref[HALO + tm:, :])
    q = _conv_silu(ext_ref, conv_ref, 0, DN_QK_W, tm)
    q_o[...] = _l2norm_heads(q, DN_DK ** -0.5).astype(q_o.dtype)
    k = _conv_silu(ext_ref, conv_ref, DN_QK_W, DN_QK_W, tm)
    k_o[...] = _l2norm_heads(k, 1.0).astype(k_o.dtype)
    v_o[...] = _conv_silu(ext_ref, conv_ref, 2 * DN_QK_W, DN_V_W, tm).astype(v_o.dtype)
    _gate_fields(proj(wba_ref), alog_ref[...], dtb_ref[...], aux_o, tm)

    z_o[...] = proj(wz_ref).astype(z_o.dtype)
    gd_o[...] = proj(wgd_ref).astype(gd_o.dtype)
    ga_o[...] = proj(wga_ref).astype(ga_o.dtype)

    cos = cos_ref[...]
    sin = sin_ref[...]
    aq = proj(waq_ref)
    for h in range(ATT_HEADS):
        sl = slice(h * HEAD_DIM, (h + 1) * HEAD_DIM)
        qh = _norm_rope(aq[:, sl], qn_ref[...], cos, sin, HEAD_DIM ** -0.5 * LOG2_E)
        aq_o[sl, :] = jnp.transpose(qh).astype(aq_o.dtype)
    ak = proj(wak_ref)
    av = proj(wav_ref)
    for h in range(ATT_KV_HEADS):
        sl = slice(h * HEAD_DIM, (h + 1) * HEAD_DIM)
        ak_o[:, sl] = _norm_rope(ak[:, sl], kn_ref[...], cos, sin, 1.0).astype(ak_o.dtype)
        av_o[sl, :] = jnp.transpose(av[:, sl]).astype(av_o.dtype)


def _in_proj(x, gain, cos, sin, qn, kn, ws, conv, alog, dtb, *, seq, tm):
    t = x.shape[0]
    nseq = seq // tm
    hb = tm // HALO
    nhb = t // HALO
    row = lambda w: pl.BlockSpec((tm, w), lambda i: (i, 0))
    tab = pl.BlockSpec((tm, HEAD_DIM), lambda i: (i % nseq, 0))
    col = lambda w: pl.BlockSpec((w, tm), lambda i: (0, i))
    sds = jax.ShapeDtypeStruct
    outs = [(row(DN_QK_W), sds((t, DN_QK_W), BF16)), (row(DN_QK_W), sds((t, DN_QK_W), BF16)),
            (row(DN_V_W), sds((t, DN_V_W), BF16)),
            (pl.BlockSpec((2, tm, AUX_W), lambda i: (0, i, 0)), sds((2, t, AUX_W), F32)),
            (row(DN_V_W), sds((t, DN_V_W), BF16)), (col(ATT_Q_W), sds((ATT_Q_W, t), BF16)),
            (row(ATT_KV_W), sds((t, ATT_KV_W), BF16)), (col(ATT_KV_W), sds((ATT_KV_W, t), BF16)),
            (row(D_MODEL), sds((t, D_MODEL), BF16)), (row(D_MODEL), sds((t, D_MODEL), BF16))]
    return pl.pallas_call(
        functools.partial(_in_proj_kernel, tm=tm, nseq=nseq),
        grid=(t // tm,),
        in_specs=[row(D_MODEL),
                  pl.BlockSpec((HALO, D_MODEL), lambda i: (jnp.maximum(i * hb - 1, 0), 0)),
                  pl.BlockSpec((HALO, D_MODEL), lambda i: (jnp.minimum((i + 1) * hb, nhb - 1), 0)),
                  _resident((1, D_MODEL)), tab, tab, _resident((1, HEAD_DIM)), _resident((1, HEAD_DIM))]
        + [_resident(w.shape) for w in ws]
        + [_resident(conv.shape), _resident(alog.shape), _resident(dtb.shape)],
        out_specs=[o[0] for o in outs],
        out_shape=[o[1] for o in outs],
        scratch_shapes=[pltpu.VMEM((tm + 2 * HALO, 2 * DN_QK_W + DN_V_W), F32)],
        compiler_params=_cparams(("parallel",)),
        name="in_proj",
    )(x, x, x, gain, cos, sin, qn, kn, *ws, conv, alog, dtb)


def _unit_lower_inverses(mats, eye, m16, m32, expand):
    each = lambda f, *ls: [f(*xs) for xs in zip(*ls)]
    a32 = [jnp.where(m32 & ~m16, a, 0.0).astype(BF16) for a in mats]
    a64 = [jnp.where(m32, 0.0, a).astype(BF16) for a in mats]
    n = [jnp.where(m16, -a, 0.0) for a in mats]
    x = [eye + ni for ni in n]
    p = each(lambda ni: _mm(ni, expand(ni)), n)
    yield
    for level in range(3):
        pe = [expand(pi) for pi in p]
        x = each(lambda xi, pei: xi + _mm(xi, pei), x, pe)
        if level < 2:
            p = each(_mm, p, pe)
        yield
    for off in (a32, a64):
        t = each(lambda oi, xi: _mm(oi, expand(xi)), off, x)
        yield
        x = each(lambda xi, ti: xi - _mm(xi, expand(ti)), x, t)
        yield
    return x


def _interleave(*gens):
    live = list(gens)
    while live:
        for g in list(live):
            try:
                next(g)
            except StopIteration:
                live.remove(g)


def _delta_rule_kernel(q_ref, k_ref, v_ref, aux_ref, o_ref, s_ref, wq_ref, u_ref, kd_ref, aqk_ref, *, cb):
    d = pl.program_id(1)
    j = pl.program_id(2)

    @pl.when(j == 0)
    def _():
        s_ref[...] = jnp.zeros_like(s_ref)

    gm = DN_GM
    blk = lambda x, s: lax.shift_right_logical(x, s)
    bd = (blk(lax.broadcasted_iota(jnp.int32, (gm, gm), 0), 6)
          == blk(lax.broadcasted_iota(jnp.int32, (gm, gm), 1), 6))
    ri = lax.broadcasted_iota(jnp.int32, (DN_CHUNK, gm), 0)
    lane = lax.broadcasted_iota(jnp.int32, (DN_CHUNK, gm), 1)
    ci = lane & (DN_CHUNK - 1)
    lblock = blk(lane, 6)
    rel = (ri - ci) * (1 - 2 * d)
    incl = rel >= 0
    strict = rel > 0
    m16 = blk(ri, 4) == blk(ci, 4)
    m32 = blk(ri, 5) == blk(ci, 5)
    eye = jnp.where(ri == ci, 1.0, 0.0).astype(F32)

    def expand(c):
        return jnp.where(bd, jnp.concatenate([c.astype(F32)] * DN_GROUP, axis=0), 0.0).astype(BF16)

    ngroups = DN_V_HEADS // DN_GROUP

    def chunk_rows(jj):
        c = jnp.where(d == 0, jj, cb - 1 - jj)
        return pl.ds(pl.multiple_of(c * DN_CHUNK, DN_CHUNK), DN_CHUNK)

    f32 = lambda x: x.astype(F32)

    def local_work(jjs):
        chains = [(jj, hg) for jj in jjs for hg in range(ngroups)]
        amats, rhss = [], []
        for jj, hg in chains:
            rows = chunk_rows(jj)
            aux = aux_ref[0, rows, :]
            h0 = hg * DN_GROUP

            def stacked(field):
                base = field * DN_V_HEADS + h0
                return jnp.concatenate([jnp.broadcast_to(aux[:, base + r:base + r + 1], (DN_CHUNK, LANES))
                                        for r in range(DN_GROUP)], axis=0)

            def blocked(st):
                wide = [jnp.concatenate([st[r * DN_CHUNK:(r + 1) * DN_CHUNK]] * 2, axis=1)
                        for r in range(DN_GROUP)]
                return jnp.where(lblock < 2, jnp.where(lblock == 0, wide[0], wide[1]),
                                 jnp.where(lblock == 2, wide[2], wide[3]))

            gc_st, beta_st = stacked(AUX_GC), stacked(AUX_BETA)
            tot_st = jnp.concatenate(
                [jnp.broadcast_to(aux[0:1, AUX_TOT * DN_V_HEADS + h0 + r:AUX_TOT * DN_V_HEADS + h0 + r + 1],
                                  (DN_CHUNK, LANES)) for r in range(DN_GROUP)], axis=0)
            egc_st = jnp.exp(gc_st)
            egl_st = jnp.exp(tot_st - gc_st)
            gc_bl = blocked(gc_st)
            gc_row = jnp.sum(jnp.where(ri == ci, gc_bl, 0.0), axis=0, keepdims=True)

            qk_heads = [(h0 + r) // (DN_V_HEADS // DN_QK_HEADS) for r in range(DN_GROUP)]
            khead = lambda ref, h: ref[rows, h * DN_DK:(h + 1) * DN_DK]
            ks = jnp.concatenate([khead(k_ref, h) for h in qk_heads], axis=0)
            qs = jnp.concatenate([khead(q_ref, h) for h in qk_heads], axis=0)
            vs = jnp.concatenate([v_ref[rows, (h0 + r) * DN_DV:(h0 + r + 1) * DN_DV]
                                  for r in range(DN_GROUP)], axis=0)

            first = lblock < 2
            gram = jnp.where(first, _mm_nt(khead(k_ref, qk_heads[0]), ks), _mm_nt(khead(k_ref, qk_heads[2]), ks))
            qk = jnp.where(first, _mm_nt(khead(q_ref, qk_heads[0]), ks), _mm_nt(khead(q_ref, qk_heads[2]), ks))
            decay = jnp.where(incl, jnp.exp(jnp.where(incl, gc_bl - gc_row, 0.0)), 0.0)
            amats.append(jnp.where(strict, blocked(beta_st) * gram * decay, 0.0))
            aqk_ref[jj, hg] = expand(qk * decay)
            rhss.append(jnp.concatenate([f32(vs) * beta_st, f32(ks) * (beta_st * egc_st)],
                                        axis=1).astype(BF16))
            kd_ref[jj, hg] = (f32(ks) * egl_st).astype(BF16)
            qd = (f32(qs) * egc_st).astype(BF16)
            for r in range(DN_GROUP):
                wq_ref[jj, h0 + r, DN_CHUNK:2 * DN_CHUNK, :] = qd[r * DN_CHUNK:(r + 1) * DN_CHUNK]
        yield
        tinvs = yield from _unit_lower_inverses(amats, eye, m16, m32, expand)
        for (jj, hg), tinv, rhs in zip(chains, tinvs, rhss):
            sol = jnp.dot(expand(tinv), rhs, preferred_element_type=F32)
            u_ref[jj, hg] = sol[:, 0:DN_DV]
            w = sol[:, DN_DV:DN_DV + DN_DK].astype(BF16)
            for r in range(DN_GROUP):
                wq_ref[jj, hg * DN_GROUP + r, 0:DN_CHUNK, :] = w[r * DN_CHUNK:(r + 1) * DN_CHUNK]
        yield

    def recurrence(jjs):
        hslice = lambda h: slice(h * DN_DV, (h + 1) * DN_DV)
        for jj in jjs:
            rows = chunk_rows(jj)
            eg_row = aux_ref[0, pl.ds(rows.start, 1), :]
            states = [s_ref[:, hslice(h)] for h in range(DN_V_HEADS)]
            res = [jnp.dot(wq_ref[jj, h], states[h].astype(BF16), preferred_element_type=F32)
                   for h in range(DN_V_HEADS)]
            yield
            vnew = []
            for h in range(DN_V_HEADS):
                hg, r = divmod(h, DN_GROUP)
                vnew.append(u_ref[jj, hg, r * DN_CHUNK:(r + 1) * DN_CHUNK, :] - res[h][0:DN_CHUNK])
            for h in range(DN_V_HEADS):
                hg, r = divmod(h, DN_GROUP)
                eg = eg_row[:, AUX_EG * DN_V_HEADS + h:AUX_EG * DN_V_HEADS + h + 1]
                kd = kd_ref[jj, hg, r * DN_CHUNK:(r + 1) * DN_CHUNK, :]
                s_ref[:, hslice(h)] = states[h] * eg + _mm_tn(kd, vnew[h])
            yield
            for hg in range(ngroups):
                hs = range(hg * DN_GROUP, (hg + 1) * DN_GROUP)
                out = (jnp.concatenate([res[h][DN_CHUNK:2 * DN_CHUNK] for h in hs], axis=0)
                       + _mm(aqk_ref[jj, hg], jnp.concatenate([vnew[h] for h in hs], axis=0)))
                for r, h in enumerate(hs):
                    o_ref[0, rows, hslice(h)] = out[r * DN_CHUNK:(r + 1) * DN_CHUNK].astype(o_ref.dtype)
            yield

    stages = [range(i, min(i + DN_PIPE, cb)) for i in range(0, cb, DN_PIPE)]
    _interleave(local_work(stages[0]))
    for prev, cur in zip(stages, stages[1:]):
        _interleave(local_work(cur), recurrence(prev))
    _interleave(recurrence(stages[-1]))


def _delta_rule(q, k, v, aux, *, batch, seq, cb):
    t = q.shape[0]
    rb = cb * DN_CHUNK
    nb = seq // rb
    ngroups = DN_V_HEADS // DN_GROUP

    def rblock(b, d, j):
        return b * nb + jnp.where(d == 0, j, nb - 1 - j)

    return pl.pallas_call(
        functools.partial(_delta_rule_kernel, cb=cb),
        grid=(batch, 2, nb),
        in_specs=[pl.BlockSpec((rb, DN_QK_W), lambda b, d, j: (rblock(b, d, j), 0)),
                  pl.BlockSpec((rb, DN_QK_W), lambda b, d, j: (rblock(b, d, j), 0)),
                  pl.BlockSpec((rb, DN_V_W), lambda b, d, j: (rblock(b, d, j), 0)),
                  pl.BlockSpec((1, rb, AUX_W), lambda b, d, j: (d, rblock(b, d, j), 0))],
        out_specs=pl.BlockSpec((1, rb, DN_V_W), lambda b, d, j: (d, rblock(b, d, j), 0)),
        out_shape=jax.ShapeDtypeStruct((2, t, DN_V_W), BF16),
        scratch_shapes=[pltpu.VMEM((DN_DK, DN_V_HEADS * DN_DV), F32),
                        pltpu.VMEM((cb, DN_V_HEADS, 2 * DN_CHUNK, DN_DK), BF16),
                        pltpu.VMEM((cb, ngroups, DN_GM, DN_DV), F32),
                        pltpu.VMEM((cb, ngroups, DN_GM, DN_DK), BF16),
                        pltpu.VMEM((cb, ngroups, DN_GM, DN_GM), BF16)],
        compiler_params=_cparams(("parallel", "parallel", "arbitrary")),
        name="delta_rule",
    )(q, k, v, aux)


def _attention_kernel(qt_ref, k_ref, vt_ref, o_ref, acc_ref, m_ref, l_ref, *, tk, seq):
    m_ref[...] = jnp.full(m_ref.shape, -jnp.inf, F32)
    l_ref[...] = jnp.zeros(l_ref.shape, F32)
    acc_ref[...] = jnp.zeros(acc_ref.shape, F32)

    def step(c, carry):
        ks = pl.ds(pl.multiple_of(c * tk, tk), tk)
        kc = k_ref[ks, :]
        vtc = vt_ref[:, ks]
        for g in range(ATT_GROUP):
            s = jnp.dot(kc, qt_ref[g * HEAD_DIM:(g + 1) * HEAD_DIM, :], preferred_element_type=F32)
            m_old = m_ref[g:g + 1, :]
            m_new = jnp.maximum(m_old, jnp.max(s, axis=0, keepdims=True))
            alpha = jnp.exp2(m_old - m_new)
            p = jnp.exp2(s - m_new)
            l_ref[g:g + 1, :] = alpha * l_ref[g:g + 1, :] + jnp.sum(p, axis=0, keepdims=True)
            m_ref[g:g + 1, :] = m_new
            acc_ref[g] = alpha * acc_ref[g] + jnp.dot(vtc, p.astype(BF16), preferred_element_type=F32)
        return carry

    nsteps = seq // tk
    lax.fori_loop(0, nsteps, step, 0, unroll=math.gcd(nsteps, ATT_UNROLL))
    for g in range(ATT_GROUP):
        out = acc_ref[g] / l_ref[g:g + 1, :]
        o_ref[:, g * HEAD_DIM:(g + 1) * HEAD_DIM] = jnp.transpose(out).astype(o_ref.dtype)


def _attention(aqt, ak, avt, *, batch, seq, tq, tk):
    t = ak.shape[0]
    nq = seq // tq
    gw = ATT_GROUP * HEAD_DIM
    return pl.pallas_call(
        functools.partial(_attention_kernel, tk=tk, seq=seq),
        grid=(batch, ATT_KV_HEADS, nq),
        in_specs=[pl.BlockSpec((gw, tq), lambda b, g, i: (g, b * nq + i)),
                  pl.BlockSpec((seq, HEAD_DIM), lambda b, g, i: (b, g)),
                  pl.BlockSpec((HEAD_DIM, seq), lambda b, g, i: (g, b))],
        out_specs=pl.BlockSpec((tq, gw), lambda b, g, i: (b * nq + i, g)),
        out_shape=jax.ShapeDtypeStruct((t, ATT_Q_W), BF16),
        scratch_shapes=[pltpu.VMEM((ATT_GROUP, HEAD_DIM, tq), F32), pltpu.VMEM((SUBLANES, tq), F32),
                        pltpu.VMEM((SUBLANES, tq), F32)],
        compiler_params=_cparams(("parallel", "parallel", "parallel")),
        name="attention",
    )(aqt, ak, avt)


def _merge_mlp_kernel(h_ref, o_ref, z_ref, att_ref, gd_ref, ga_ref, pe_ref, on_ref, nm_ref, np_ref,
                      wdn_ref, wat_ref, wo_ref, wup_ref, wdown_ref, wpg_ref, wple_ref, out_ref):
    dot = lambda a, w_ref: jnp.dot(a, w_ref[...], preferred_element_type=F32)
    o = o_ref[0].astype(F32) + o_ref[1].astype(F32)
    z = z_ref[...].astype(F32)
    parts = []
    for h in range(DN_V_HEADS):
        sl = slice(h * DN_DV, (h + 1) * DN_DV)
        parts.append((_rms(o[:, sl], on_ref[...]) * _silu(z[:, sl])).astype(BF16))
    y_dn = dot(jnp.concatenate(parts, axis=1), wdn_ref)
    y_at = dot(att_ref[...], wat_ref)
    merged = _sigmoid(gd_ref[...].astype(F32)) * y_dn + _sigmoid(ga_ref[...].astype(F32)) * y_at
    h1 = h_ref[...] + dot(merged.astype(BF16), wo_ref)
    hm = dot(_rms(h1, nm_ref[...]).astype(BF16), wup_ref)
    h2 = h1 + dot(jnp.square(jnp.maximum(hm, 0.0)).astype(BF16), wdown_ref)
    gate = _sigmoid(dot(_rms(h2, np_ref[...]).astype(BF16), wpg_ref))
    out_ref[...] = h2 + gate * dot(pe_ref[...].astype(BF16), wple_ref)


def _merge_mlp(h, o, z, att, gd, ga, pe, on, nm, npl, wdn, wat, wo, wup, wdown, wpg, wple, *, tm):
    t = h.shape[0]
    row = lambda w: pl.BlockSpec((tm, w), lambda i: (i, 0))
    consts = (on, nm, npl, wdn, wat, wo, wup, wdown, wpg, wple)
    return pl.pallas_call(
        _merge_mlp_kernel,
        grid=(t // tm,),
        in_specs=[row(D_MODEL), pl.BlockSpec((2, tm, DN_V_W), lambda i: (0, i, 0)), row(DN_V_W),
                  row(ATT_Q_W), row(D_MODEL), row(D_MODEL), row(PLE_DIM)]
        + [_resident(c.shape) for c in consts],
        out_specs=row(D_MODEL),
        out_shape=jax.ShapeDtypeStruct((t, D_MODEL), F32),
        compiler_params=_cparams(("parallel",)),
        name="merge_mlp",
    )(h, o, z, att, gd, ga, pe, *consts)


def _rope_tables(seq):
    pos = jnp.arange(seq)
    r = (pos // GRID_W).astype(F32)
    c = (pos % GRID_W).astype(F32)
    half = HEAD_DIM // 2
    inv = jnp.power(jnp.float32(ROPE_THETA), -jnp.arange(0, half, 2, dtype=F32) / half)
    ar = r[:, None] * inv
    ac = c[:, None] * inv
    cos = jnp.concatenate([jnp.cos(ar), jnp.cos(ar), jnp.cos(ac), jnp.cos(ac)], axis=1)
    sin = jnp.concatenate([-jnp.sin(ar), jnp.sin(ar), -jnp.sin(ac), jnp.sin(ac)], axis=1)
    return cos, sin


def _prepare_weights(norm_mix, w_in, dn_conv_w, dn_a_log, dn_dt_bias, dn_out_norm, w_dn_branch,
                     q_norm, k_norm, w_attn_branch, w_o, norm_mlp, w_up, w_down, norm_ple,
                     w_ple_gate, w_ple):
    bf = lambda w: w.astype(BF16)
    w_in = w_in[0]
    splits = (2 * DN_QK_W + DN_V_W, DN_V_W, 2 * DN_V_HEADS, 2 * DN_V_HEADS, ATT_Q_W, ATT_KV_W, ATT_KV_W,
              D_MODEL, D_MODEL)
    cols, start = [], 0
    for s in splits:
        cols.append(w_in[:, start:start + s])
        start += s
    w_qkv, w_z, w_b, w_a, w_aq, w_ak, w_av, w_gd, w_ga = cols
    pad = jnp.zeros((D_MODEL, LANES - 2 * DN_V_HEADS), F32)
    w_ba = jnp.concatenate([w_b, pad, w_a, pad], axis=1)
    in_ws = tuple(bf(w) for w in (w_qkv, w_z, w_ba, w_aq, w_ak, w_av, w_gd, w_ga))
    lane_pad = lambda x: jnp.pad(x.reshape(1, -1), ((0, 0), (0, LANES - x.size)))
    return dict(
        norm_mix=norm_mix[0].reshape(1, -1), in_ws=in_ws,
        conv=dn_conv_w[0],
        alog=lane_pad(dn_a_log[0]), dtb=lane_pad(dn_dt_bias[0]),
        out_norm=dn_out_norm[0].reshape(1, -1), w_dn=bf(w_dn_branch[0]),
        q_norm=q_norm[0].reshape(1, -1), k_norm=k_norm[0].reshape(1, -1),
        w_at=bf(w_attn_branch[0]), w_o=bf(w_o[0]),
        norm_mlp=norm_mlp[0].reshape(1, -1), w_up=bf(w_up[0]), w_down=bf(w_down[0]),
        norm_ple=norm_ple[0].reshape(1, -1), w_pg=bf(w_ple_gate[0]), w_ple=bf(w_ple[0]))


def _tiles(seq):
    return dict(tm_proj=min(256, seq), cb=min(16, seq // DN_CHUNK),
                tq=min(256, seq), tk=min(128, seq), tm_merge=min(256, seq))


def _encoder_layer(x, pe, p):
    batch, seq, _ = x.shape
    assert seq % DN_CHUNK == 0 and seq % GRID_W == 0
    t = batch * seq
    tl = _tiles(seq)
    h = x.reshape(t, D_MODEL)
    cos, sin = _rope_tables(seq)
    qn, kn, vc, aux, z, aq, ak, av, gd, ga = _in_proj(
        h, p["norm_mix"], cos, sin, p["q_norm"], p["k_norm"], p["in_ws"], p["conv"], p["alog"], p["dtb"],
        seq=seq, tm=tl["tm_proj"])
    o = _delta_rule(qn, kn, vc, aux, batch=batch, seq=seq, cb=tl["cb"])
    att = _attention(aq, ak, av, batch=batch, seq=seq, tq=tl["tq"], tk=tl["tk"])
    h3 = _merge_mlp(h, o, z, att, gd, ga, pe.reshape(t, PLE_DIM), p["out_norm"], p["norm_mlp"], p["norm_ple"],
                    p["w_dn"], p["w_at"], p["w_o"], p["w_up"], p["w_down"], p["w_pg"], p["w_ple"],
                    tm=tl["tm_merge"])
    return h3.reshape(batch, seq, D_MODEL)


@jax.jit
def kernel(x_prompt, x_sample, p_prompt, p_sample, norm_mix, w_in, dn_conv_w, dn_a_log, dn_dt_bias,
           dn_out_norm, w_dn_branch, q_norm, k_norm, w_attn_branch, w_o, norm_mlp, w_up, w_down,
           norm_ple, w_ple_gate, w_ple):
    assert norm_mix.shape[0] == 1, "single-layer trunk"
    p = _prepare_weights(norm_mix, w_in, dn_conv_w, dn_a_log, dn_dt_bias, dn_out_norm, w_dn_branch,
                         q_norm, k_norm, w_attn_branch, w_o, norm_mlp, w_up, w_down, norm_ple,
                         w_ple_gate, w_ple)
    y_prompt = _encoder_layer(x_prompt, p_prompt[0], p)
    y_sample = _encoder_layer(x_sample, p_sample[0], p)
    return (y_prompt, y_sample)
```

```python
import functools
import math

import jax
import jax.numpy as jnp
from jax import lax
from jax.experimental import pallas as pl
from jax.experimental.pallas import tpu as pltpu

F32 = jnp.float32
BF16 = jnp.bfloat16

D_MODEL = 1024
PLE_DIM = 256
D_FF = 4 * D_MODEL
NORM_EPS = 1e-6
GRID_W = 64
DN_QK_HEADS = 4
DN_V_HEADS = 8
DN_DK = 128
DN_DV = 128
DN_CHUNK = 64
DN_CONV = 5
DN_QK_W = DN_QK_HEADS * DN_DK
DN_V_W = DN_V_HEADS * DN_DV
ATT_HEADS = 8
ATT_KV_HEADS = 2
ATT_GROUP = ATT_HEADS // ATT_KV_HEADS
HEAD_DIM = 128
ROPE_THETA = 10000.0
LOG2_E = math.log2(math.e)
ATT_Q_W = ATT_HEADS * HEAD_DIM
ATT_KV_W = ATT_KV_HEADS * HEAD_DIM
ATT_UNROLL = 32

LANES = 128
SUBLANES = 8
VMEM_LIMIT_BYTES = 56 * 1024 * 1024

DN_GROUP = 4
DN_GM = DN_GROUP * DN_CHUNK
DN_PIPE = 2
AUX_W = LANES
AUX_GC, AUX_BETA, AUX_TOT, AUX_EG = 0, 1, 2, 3


def _cparams(semantics):
    return pltpu.CompilerParams(dimension_semantics=semantics, vmem_limit_bytes=VMEM_LIMIT_BYTES)


def _resident(shape):
    nd = len(shape)
    return pl.BlockSpec(shape, lambda *_: (0,) * nd, pipeline_mode=pl.Buffered(1))


def _rms(x, gain):
    ms = jnp.mean(x * x, axis=-1, keepdims=True)
    return x * lax.rsqrt(ms + NORM_EPS) * gain


def _mm(a, b):
    return jnp.dot(a.astype(BF16), b.astype(BF16), preferred_element_type=F32)


def _mm_nt(a, b):
    return lax.dot_general(a.astype(BF16), b.astype(BF16), (((1,), (1,)), ((), ())),
                           preferred_element_type=F32)


def _mm_tn(a, b):
    return lax.dot_general(a.astype(BF16), b.astype(BF16), (((0,), (0,)), ((), ())),
                           preferred_element_type=F32)


def _sigmoid(x):
    return 1.0 / (1.0 + jnp.exp(-x))


def _silu(x):
    return x * _sigmoid(x)


def _swap32(x):
    lane = lax.broadcasted_iota(jnp.int32, x.shape, 1)
    first = (lane & 32) == 0
    return jnp.where(first, pltpu.roll(x, LANES - 32, 1), pltpu.roll(x, 32, 1))


def _norm_rope(x, gain, cos, sin, scale):
    y = _rms(x, gain)
    return (y * cos + _swap32(y) * sin) * scale


HALO = 16


def _conv_silu(ext_ref, w_ref, col0, width, tm):
    pad = DN_CONV // 2
    acc = None
    for tap in range(DN_CONV):
        start = HALO - pad + tap
        term = ext_ref[start:start + tm, col0:col0 + width] * w_ref[tap:tap + 1, col0:col0 + width]
        acc = term if acc is None else acc + term
    return _silu(acc)


def _split3(x):
    hi = x.astype(BF16)
    r1 = x - hi.astype(F32)
    mid = r1.astype(BF16)
    lo = (r1 - mid.astype(F32)).astype(BF16)
    return hi, mid, lo


def _ones_mm(mask_bf, x):
    hi, mid, lo = _split3(x)
    dot = lambda p: jnp.dot(mask_bf, p, preferred_element_type=F32)
    return dot(hi) + dot(mid) + dot(lo)


def _place(x, src, dst):
    shift = (dst - src) % LANES
    return x if shift == 0 else pltpu.roll(x, shift, 1)


def _gate_fields(ba, alog, dtb, aux_o, tm):
    beta = _sigmoid(ba[:, 0:LANES])
    a = ba[:, LANES:2 * LANES] + dtb
    softplus = jnp.maximum(a, 0.0) + jnp.log(1.0 + jnp.exp(-jnp.abs(a)))
    g = -jnp.exp(alog) * softplus

    ri = lax.broadcasted_iota(jnp.int32, (tm, tm), 0)
    ci = lax.broadcasted_iota(jnp.int32, (tm, tm), 1)
    same = lax.shift_right_logical(ri, 6) == lax.shift_right_logical(ci, 6)
    one = lambda m: jnp.where(m, 1.0, 0.0).astype(BF16)
    pre = _ones_mm(one(same & (ci <= ri)), g)
    suf = _ones_mm(one(same & (ci >= ri)), g)
    tot = _ones_mm(one(same), g)
    lane = lax.broadcasted_iota(jnp.int32, (tm, LANES), 1)
    gc = jnp.where(lane < DN_V_HEADS, pre, suf)
    eg = jnp.exp(tot)
    for d in range(2):
        src = d * DN_V_HEADS
        aux = jnp.zeros((tm, AUX_W), F32)
        for field, val in ((AUX_GC, gc), (AUX_BETA, beta), (AUX_TOT, tot), (AUX_EG, eg)):
            dst = field * DN_V_HEADS
            aux = jnp.where((lane >= dst) & (lane < dst + DN_V_HEADS), _place(val, src, dst), aux)
        aux_o[d] = aux


def _in_proj_kernel(x_ref, xp_ref, xn_ref, gain_ref, cos_ref, sin_ref, qn_ref, kn_ref,
                    wqkv_ref, wz_ref, wba_ref, waq_ref, wak_ref, wav_ref, wgd_ref, wga_ref,
                    conv_ref, alog_ref, dtb_ref,
                    q_o, k_o, v_o, aux_o, z_o, aq_o, ak_o, av_o, gd_o, ga_o, ext_ref, *, tm, nseq):
    i = pl.program_id(0)
    first = (i % nseq) == 0
    last = (i % nseq) == nseq - 1
    gain = gain_ref[...]
    norm = lambda ref: _rms(ref[...], gain).astype(BF16)
    u = norm(x_ref)

    def proj(w_ref):
        return jnp.dot(u, w_ref[...], preferred_element_type=F32)

    u_ext = jnp.concatenate([norm(xp_ref), u, norm(xn_ref)], axis=0)
    cos = cos_ref[...]
    sin = sin_ref[...]
    cbw = 2 * LANES

    def dn_block(b):
        cs = slice(b * cbw, (b + 1) * cbw)
        e = jnp.dot(u_ext, wqkv_ref[:, cs], preferred_element_type=F32)
        ext_ref[0:HALO, cs] = jnp.where(first, 0.0, e[0:HALO])
        ext_ref[HALO:HALO + tm, cs] = e[HALO:HALO + tm]
        ext_ref[HALO + tm:, cs] = jnp.where(last, 0.0, e[HALO + tm:])
        y = _conv_silu(ext_ref, conv_ref, b * cbw, cbw, tm)
        if b * cbw < 2 * DN_QK_W:
            scale = DN_DK ** -0.5 if b * cbw < DN_QK_W else 1.0
            out = q_o if b * cbw < DN_QK_W else k_o
            c0 = (b * cbw) % DN_QK_W
            for hh in range(cbw // DN_DK):
                yh = y[:, hh * DN_DK:(hh + 1) * DN_DK]
                r = lax.rsqrt(jnp.sum(yh * yh, axis=-1, keepdims=True) + NORM_EPS)
                out[:, c0 + hh * DN_DK:c0 + (hh + 1) * DN_DK] = (yh * (r * scale)).astype(out.dtype)
        else:
            c0 = b * cbw - 2 * DN_QK_W
            v_o[:, c0:c0 + cbw] = y.astype(v_o.dtype)

    def plain_block(w_ref, o_ref, b):
        cs = slice(b * cbw, (b + 1) * cbw)
        o_ref[:, cs] = jnp.dot(u, w_ref[:, cs], preferred_element_type=F32).astype(o_ref.dtype)

    def aq_block(b):
        cs = slice(b * cbw, (b + 1) * cbw)
        a = jnp.dot(u, waq_ref[:, cs], preferred_element_type=F32)
        for hh in range(cbw // HEAD_DIM):
            qh = _norm_rope(a[:, hh * HEAD_DIM:(hh + 1) * HEAD_DIM], qn_ref[...], cos, sin,
                            HEAD_DIM ** -0.5 * LOG2_E)
            r0 = b * cbw + hh * HEAD_DIM
            aq_o[r0:r0 + HEAD_DIM, :] = jnp.transpose(qh).astype(aq_o.dtype)

    def kv_block():
        ak = proj(wak_ref)
        av = proj(wav_ref)
        for h in range(ATT_KV_HEADS):
            sl = slice(h * HEAD_DIM, (h + 1) * HEAD_DIM)
            ak_o[:, sl] = _norm_rope(ak[:, sl], kn_ref[...], cos, sin, 1.0).astype(ak_o.dtype)
            av_o[sl, :] = jnp.transpose(av[:, sl]).astype(av_o.dtype)

    others = ([functools.partial(plain_block, wz_ref, z_o, b) for b in range(DN_V_W // cbw)]
              + [functools.partial(aq_block, b) for b in range(ATT_Q_W // cbw)]
              + [functools.partial(plain_block, wgd_ref, gd_o, b) for b in range(D_MODEL // cbw)]
              + [functools.partial(plain_block, wga_ref, ga_o, b) for b in range(D_MODEL // cbw)]
              + [kv_block, lambda: _gate_fields(proj(wba_ref), alog_ref[...], dtb_ref[...], aux_o, tm)])
    n_dn = (2 * DN_QK_W + DN_V_W) // cbw
    per = -(-len(others) // n_dn)
    for b in range(n_dn):
        dn_block(b)
        for f in others[b * per:(b + 1) * per]:
            f()


def _in_proj(x, gain, cos, sin, qn, kn, ws, conv, alog, dtb, *, seq, tm):
    t = x.shape[0]
    nseq = seq // tm
    hb = tm // HALO
    nhb = t // HALO
    row = lambda w: pl.BlockSpec((tm, w), lambda i: (i, 0))
    tab = pl.BlockSpec((tm, HEAD_DIM), lambda i: (i % nseq, 0))
    col = lambda w: pl.BlockSpec((w, tm), lambda i: (0, i))
    sds = jax.ShapeDtypeStruct
    outs = [(row(DN_QK_W), sds((t, DN_QK_W), BF16)), (row(DN_QK_W), sds((t, DN_QK_W), BF16)),
            (row(DN_V_W), sds((t, DN_V_W), BF16)),
            (pl.BlockSpec((2, tm, AUX_W), lambda i: (0, i, 0)), sds((2, t, AUX_W), F32)),
            (row(DN_V_W), sds((t, DN_V_W), BF16)), (col(ATT_Q_W), sds((ATT_Q_W, t), BF16)),
            (row(ATT_KV_W), sds((t, ATT_KV_W), BF16)), (col(ATT_KV_W), sds((ATT_KV_W, t), BF16)),
            (row(D_MODEL), sds((t, D_MODEL), BF16)), (row(D_MODEL), sds((t, D_MODEL), BF16))]
    return pl.pallas_call(
        functools.partial(_in_proj_kernel, tm=tm, nseq=nseq),
        grid=(t // tm,),
        in_specs=[row(D_MODEL),
                  pl.BlockSpec((HALO, D_MODEL), lambda i: (jnp.maximum(i * hb - 1, 0), 0)),
                  pl.BlockSpec((HALO, D_MODEL), lambda i: (jnp.minimum((i + 1) * hb, nhb - 1), 0)),
                  _resident((1, D_MODEL)), tab, tab, _resident((1, HEAD_DIM)), _resident((1, HEAD_DIM))]
        + [_resident(w.shape) for w in ws]
        + [_resident(conv.shape), _resident(alog.shape), _resident(dtb.shape)],
        out_specs=[o[0] for o in outs],
        out_shape=[o[1] for o in outs],
        scratch_shapes=[pltpu.VMEM((tm + 2 * HALO, 2 * DN_QK_W + DN_V_W), F32)],
        compiler_params=_cparams(("parallel",)),
        name="in_proj",
    )(x, x, x, gain, cos, sin, qn, kn, *ws, conv, alog, dtb)


def _unit_lower_inverses(mats, eye, m16, m32, expand):
    each = lambda f, *ls: [f(*xs) for xs in zip(*ls)]
    a32 = [jnp.where(m32 & ~m16, a, 0.0).astype(BF16) for a in mats]
    a64 = [jnp.where(m32, 0.0, a).astype(BF16) for a in mats]
    n = [jnp.where(m16, -a, 0.0) for a in mats]
    x = [eye + ni for ni in n]
    p = each(lambda ni: _mm(ni, expand(ni)), n)
    yield
    for level in range(3):
        pe = [expand(pi) for pi in p]
        x = each(lambda xi, pei: xi + _mm(xi, pei), x, pe)
        if level < 2:
            p = each(_mm, p, pe)
        yield
    for off in (a32, a64):
        t = each(lambda oi, xi: _mm(oi, expand(xi)), off, x)
        yield
        x = each(lambda xi, ti: xi - _mm(xi, expand(ti)), x, t)
        yield
    return x


def _interleave(*gens):
    live = list(gens)
    while live:
        for g in list(live):
            try:
                next(g)
            except StopIteration:
                live.remove(g)


def _delta_rule_kernel(q_ref, k_ref, v_ref, aux_ref, o_ref, s_ref, wq_ref, u_ref, kd_ref, aqk_ref, *, cb):
    d = pl.program_id(1)
    j = pl.program_id(2)

    @pl.when(j == 0)
    def _():
        s_ref[...] = jnp.zeros_like(s_ref)

    gm = DN_GM
    blk = lambda x, s: lax.shift_right_logical(x, s)
    bd = (blk(lax.broadcasted_iota(jnp.int32, (gm, gm), 0), 6)
          == blk(lax.broadcasted_iota(jnp.int32, (gm, gm), 1), 6))
    ri = lax.broadcasted_iota(jnp.int32, (DN_CHUNK, gm), 0)
    lane = lax.broadcasted_iota(jnp.int32, (DN_CHUNK, gm), 1)
    ci = lane & (DN_CHUNK - 1)
    lblock = blk(lane, 6)
    rel = (ri - ci) * (1 - 2 * d)
    incl = rel >= 0
    strict = rel > 0
    m16 = blk(ri, 4) == blk(ci, 4)
    m32 = blk(ri, 5) == blk(ci, 5)
    eye = jnp.where(ri == ci, 1.0, 0.0).astype(F32)

    def expand(c):
        return jnp.where(bd, jnp.concatenate([c.astype(F32)] * DN_GROUP, axis=0), 0.0).astype(BF16)

    ngroups = DN_V_HEADS // DN_GROUP

    def chunk_rows(jj):
        c = jnp.where(d == 0, jj, cb - 1 - jj)
        return pl.ds(pl.multiple_of(c * DN_CHUNK, DN_CHUNK), DN_CHUNK)

    f32 = lambda x: x.astype(F32)

    def local_work(jjs):
        chains = [(jj, hg) for jj in jjs for hg in range(ngroups)]
        amats, rhss = [], []
        for jj, hg in chains:
            rows = chunk_rows(jj)
            aux = aux_ref[0, rows, :]
            h0 = hg * DN_GROUP

            def stacked(field):
                base = field * DN_V_HEADS + h0
                return jnp.concatenate([jnp.broadcast_to(aux[:, base + r:base + r + 1], (DN_CHUNK, LANES))
                                        for r in range(DN_GROUP)], axis=0)

            def blocked(st):
                wide = [jnp.concatenate([st[r * DN_CHUNK:(r + 1) * DN_CHUNK]] * 2, axis=1)
                        for r in range(DN_GROUP)]
                return jnp.where(lblock < 2, jnp.where(lblock == 0, wide[0], wide[1]),
                                 jnp.where(lblock == 2, wide[2], wide[3]))

            gc_st, beta_st = stacked(AUX_GC), stacked(AUX_BETA)
            tot_st = jnp.concatenate(
                [jnp.broadcast_to(aux[0:1, AUX_TOT * DN_V_HEADS + h0 + r:AUX_TOT * DN_V_HEADS + h0 + r + 1],
                                  (DN_CHUNK, LANES)) for r in range(DN_GROUP)], axis=0)
            egc_st = jnp.exp(gc_st)
            egl_st = jnp.exp(tot_st - gc_st)
            gc_bl = blocked(gc_st)
            gc_row = jnp.sum(jnp.where(ri == ci, gc_bl, 0.0), axis=0, keepdims=True)

            qk_heads = [(h0 + r) // (DN_V_HEADS // DN_QK_HEADS) for r in range(DN_GROUP)]
            khead = lambda ref, h: ref[rows, h * DN_DK:(h + 1) * DN_DK]
            ks = jnp.concatenate([khead(k_ref, h) for h in qk_heads], axis=0)
            qs = jnp.concatenate([khead(q_ref, h) for h in qk_heads], axis=0)
            vs = jnp.concatenate([v_ref[rows, (h0 + r) * DN_DV:(h0 + r + 1) * DN_DV]
                                  for r in range(DN_GROUP)], axis=0)

            first = lblock < 2
            gram = jnp.where(first, _mm_nt(khead(k_ref, qk_heads[0]), ks), _mm_nt(khead(k_ref, qk_heads[2]), ks))
            qk = jnp.where(first, _mm_nt(khead(q_ref, qk_heads[0]), ks), _mm_nt(khead(q_ref, qk_heads[2]), ks))
            decay = jnp.where(incl, jnp.exp(jnp.where(incl, gc_bl - gc_row, 0.0)), 0.0)
            amats.append(jnp.where(strict, blocked(beta_st) * gram * decay, 0.0))
            aqk_ref[jj, hg] = expand(qk * decay)
            rhss.append(jnp.concatenate([f32(vs) * beta_st, f32(ks) * (beta_st * egc_st)],
                                        axis=1).astype(BF16))
            kd_ref[jj, hg] = (f32(ks) * egl_st).astype(BF16)
            qd = (f32(qs) * egc_st).astype(BF16)
            for r in range(DN_GROUP):
                wq_ref[jj, h0 + r, DN_CHUNK:2 * DN_CHUNK, :] = qd[r * DN_CHUNK:(r + 1) * DN_CHUNK]
        yield
        tinvs = yield from _unit_lower_inverses(amats, eye, m16, m32, expand)
        for (jj, hg), tinv, rhs in zip(chains, tinvs, rhss):
            sol = jnp.dot(expand(tinv), rhs, preferred_element_type=F32)
            u_ref[jj, hg] = sol[:, 0:DN_DV]
            w = sol[:, DN_DV:DN_DV + DN_DK].astype(BF16)
            for r in range(DN_GROUP):
                wq_ref[jj, hg * DN_GROUP + r, 0:DN_CHUNK, :] = w[r * DN_CHUNK:(r + 1) * DN_CHUNK]
        yield

    def recurrence(jjs):
        hslice = lambda h: slice(h * DN_DV, (h + 1) * DN_DV)
        for jj in jjs:
            rows = chunk_rows(jj)
            eg_row = aux_ref[0, pl.ds(rows.start, 1), :]
            states = [s_ref[:, hslice(h)] for h in range(DN_V_HEADS)]
            res = [jnp.dot(wq_ref[jj, h], states[h].astype(BF16), preferred_element_type=F32)
                   for h in range(DN_V_HEADS)]
            yield
            vnew = []
            for h in range(DN_V_HEADS):
                hg, r = divmod(h, DN_GROUP)
                vnew.append(u_ref[jj, hg, r * DN_CHUNK:(r + 1) * DN_CHUNK, :] - res[h][0:DN_CHUNK])
            for h in range(DN_V_HEADS):
                hg, r = divmod(h, DN_GROUP)
                eg = eg_row[:, AUX_EG * DN_V_HEADS + h:AUX_EG * DN_V_HEADS + h + 1]
                kd = kd_ref[jj, hg, r * DN_CHUNK:(r + 1) * DN_CHUNK, :]
                s_ref[:, hslice(h)] = states[h] * eg + _mm_tn(kd, vnew[h])
            yield
            for hg in range(ngroups):
                hs = range(hg * DN_GROUP, (hg + 1) * DN_GROUP)
                out = (jnp.concatenate([res[h][DN_CHUNK:2 * DN_CHUNK] for h in hs], axis=0)
                       + _mm(aqk_ref[jj, hg], jnp.concatenate([vnew[h] for h in hs], axis=0)))
                for r, h in enumerate(hs):
                    o_ref[0, rows, hslice(h)] = out[r * DN_CHUNK:(r + 1) * DN_CHUNK].astype(o_ref.dtype)
            yield

    stages = [range(i, min(i + DN_PIPE, cb)) for i in range(0, cb, DN_PIPE)]
    _interleave(local_work(stages[0]))
    for prev, cur in zip(stages, stages[1:]):
        _interleave(local_work(cur), recurrence(prev))
    _interleave(recurrence(stages[-1]))


def _delta_rule(q, k, v, aux, *, batch, seq, cb):
    t = q.shape[0]
    rb = cb * DN_CHUNK
    nb = seq // rb
    ngroups = DN_V_HEADS // DN_GROUP

    def rblock(b, d, j):
        return b * nb + jnp.where(d == 0, j, nb - 1 - j)

    return pl.pallas_call(
        functools.partial(_delta_rule_kernel, cb=cb),
        grid=(batch, 2, nb),
        in_specs=[pl.BlockSpec((rb, DN_QK_W), lambda b, d, j: (rblock(b, d, j), 0)),
                  pl.BlockSpec((rb, DN_QK_W), lambda b, d, j: (rblock(b, d, j), 0)),
                  pl.BlockSpec((rb, DN_V_W), lambda b, d, j: (rblock(b, d, j), 0)),
                  pl.BlockSpec((1, rb, AUX_W), lambda b, d, j: (d, rblock(b, d, j), 0))],
        out_specs=pl.BlockSpec((1, rb, DN_V_W), lambda b, d, j: (d, rblock(b, d, j), 0)),
        out_shape=jax.ShapeDtypeStruct((2, t, DN_V_W), BF16),
        scratch_shapes=[pltpu.VMEM((DN_DK, DN_V_HEADS * DN_DV), F32),
                        pltpu.VMEM((cb, DN_V_HEADS, 2 * DN_CHUNK, DN_DK), BF16),
                        pltpu.VMEM((cb, ngroups, DN_GM, DN_DV), F32),
                        pltpu.VMEM((cb, ngroups, DN_GM, DN_DK), BF16),
                        pltpu.VMEM((cb, ngroups, DN_GM, DN_GM), BF16)],
        compiler_params=_cparams(("parallel", "parallel", "arbitrary")),
        name="delta_rule",
    )(q, k, v, aux)


def _attention_kernel(qt_ref, k_ref, vt_ref, o_ref, acc_ref, m_ref, l_ref, *, tk, seq):
    m_ref[...] = jnp.full(m_ref.shape, -jnp.inf, F32)
    l_ref[...] = jnp.zeros(l_ref.shape, F32)
    acc_ref[...] = jnp.zeros(acc_ref.shape, F32)

    def step(c, carry):
        ks = pl.ds(pl.multiple_of(c * tk, tk), tk)
        kc = k_ref[ks, :]
        vtc = vt_ref[:, ks]
        for g in range(ATT_GROUP):
            s = jnp.dot(kc, qt_ref[g * HEAD_DIM:(g + 1) * HEAD_DIM, :], preferred_element_type=F32)
            m_old = m_ref[g:g + 1, :]
            m_new = jnp.maximum(m_old, jnp.max(s, axis=0, keepdims=True))
            alpha = jnp.exp2(m_old - m_new)
            p = jnp.exp2(s - m_new)
            l_ref[g:g + 1, :] = alpha * l_ref[g:g + 1, :] + jnp.sum(p, axis=0, keepdims=True)
            m_ref[g:g + 1, :] = m_new
            acc_ref[g] = alpha * acc_ref[g] + jnp.dot(vtc, p.astype(BF16), preferred_element_type=F32)
        return carry

    nsteps = seq // tk
    lax.fori_loop(0, nsteps, step, 0, unroll=math.gcd(nsteps, ATT_UNROLL))
    for g in range(ATT_GROUP):
        out = acc_ref[g] / l_ref[g:g + 1, :]
        o_ref[:, g * HEAD_DIM:(g + 1) * HEAD_DIM] = jnp.transpose(out).astype(o_ref.dtype)


def _attention(aqt, ak, avt, *, batch, seq, tq, tk):
    t = ak.shape[0]
    nq = seq // tq
    gw = ATT_GROUP * HEAD_DIM
    return pl.pallas_call(
        functools.partial(_attention_kernel, tk=tk, seq=seq),
        grid=(batch, ATT_KV_HEADS, nq),
        in_specs=[pl.BlockSpec((gw, tq), lambda b, g, i: (g, b * nq + i)),
                  pl.BlockSpec((seq, HEAD_DIM), lambda b, g, i: (b, g)),
                  pl.BlockSpec((HEAD_DIM, seq), lambda b, g, i: (g, b))],
        out_specs=pl.BlockSpec((tq, gw), lambda b, g, i: (b * nq + i, g)),
        out_shape=jax.ShapeDtypeStruct((t, ATT_Q_W), BF16),
        scratch_shapes=[pltpu.VMEM((ATT_GROUP, HEAD_DIM, tq), F32), pltpu.VMEM((SUBLANES, tq), F32),
                        pltpu.VMEM((SUBLANES, tq), F32)],
        compiler_params=_cparams(("parallel", "parallel", "parallel")),
        name="attention",
    )(aqt, ak, avt)


def _merge_mlp_kernel(h_ref, o_ref, z_ref, att_ref, gd_ref, ga_ref, pe_ref, on_ref, nm_ref, np_ref,
                      wdn_ref, wat_ref, wo_ref, wup_ref, wdown_ref, wpg_ref, wple_ref, out_ref):
    dot = lambda a, w_ref: jnp.dot(a, w_ref[...], preferred_element_type=F32)
    o = o_ref[0].astype(F32) + o_ref[1].astype(F32)
    z = z_ref[...].astype(F32)
    parts = []
    for h in range(DN_V_HEADS):
        sl = slice(h * DN_DV, (h + 1) * DN_DV)
        parts.append((_rms(o[:, sl], on_ref[...]) * _silu(z[:, sl])).astype(BF16))
    y_dn = dot(jnp.concatenate(parts, axis=1), wdn_ref)
    y_at = dot(att_ref[...], wat_ref)
    merged = _sigmoid(gd_ref[...].astype(F32)) * y_dn + _sigmoid(ga_ref[...].astype(F32)) * y_at
    h1 = h_ref[...] + dot(merged.astype(BF16), wo_ref)
    hm = dot(_rms(h1, nm_ref[...]).astype(BF16), wup_ref)
    h2 = h1 + dot(jnp.square(jnp.maximum(hm, 0.0)).astype(BF16), wdown_ref)
    gate = _sigmoid(dot(_rms(h2, np_ref[...]).astype(BF16), wpg_ref))
    out_ref[...] = h2 + gate * dot(pe_ref[...].astype(BF16), wple_ref)


def _merge_mlp(h, o, z, att, gd, ga, pe, on, nm, npl, wdn, wat, wo, wup, wdown, wpg, wple, *, tm):
    t = h.shape[0]
    row = lambda w: pl.BlockSpec((tm, w), lambda i: (i, 0))
    consts = (on, nm, npl, wdn, wat, wo, wup, wdown, wpg, wple)
    return pl.pallas_call(
        _merge_mlp_kernel,
        grid=(t // tm,),
        in_specs=[row(D_MODEL), pl.BlockSpec((2, tm, DN_V_W), lambda i: (0, i, 0)), row(DN_V_W),
                  row(ATT_Q_W), row(D_MODEL), row(D_MODEL), row(PLE_DIM)]
        + [_resident(c.shape) for c in consts],
        out_specs=row(D_MODEL),
        out_shape=jax.ShapeDtypeStruct((t, D_MODEL), F32),
        compiler_params=_cparams(("parallel",)),
        name="merge_mlp",
    )(h, o, z, att, gd, ga, pe, *consts)


def _rope_tables(seq):
    pos = jnp.arange(seq)
    r = (pos // GRID_W).astype(F32)
    c = (pos % GRID_W).astype(F32)
    half = HEAD_DIM // 2
    inv = jnp.power(jnp.float32(ROPE_THETA), -jnp.arange(0, half, 2, dtype=F32) / half)
    ar = r[:, None] * inv
    ac = c[:, None] * inv
    cos = jnp.concatenate([jnp.cos(ar), jnp.cos(ar), jnp.cos(ac), jnp.cos(ac)], axis=1)
    sin = jnp.concatenate([-jnp.sin(ar), jnp.sin(ar), -jnp.sin(ac), jnp.sin(ac)], axis=1)
    return cos, sin


def _prepare_weights(norm_mix, w_in, dn_conv_w, dn_a_log, dn_dt_bias, dn_out_norm, w_dn_branch,
                     q_norm, k_norm, w_attn_branch, w_o, norm_mlp, w_up, w_down, norm_ple,
                     w_ple_gate, w_ple):
    bf = lambda w: w.astype(BF16)
    w_in = w_in[0]
    splits = (2 * DN_QK_W + DN_V_W, DN_V_W, 2 * DN_V_HEADS, 2 * DN_V_HEADS, ATT_Q_W, ATT_KV_W, ATT_KV_W,
              D_MODEL, D_MODEL)
    cols, start = [], 0
    for s in splits:
        cols.append(w_in[:, start:start + s])
        start += s
    w_qkv, w_z, w_b, w_a, w_aq, w_ak, w_av, w_gd, w_ga = cols
    pad = jnp.zeros((D_MODEL, LANES - 2 * DN_V_HEADS), F32)
    w_ba = jnp.concatenate([w_b, pad, w_a, pad], axis=1)
    in_ws = tuple(bf(w) for w in (w_qkv, w_z, w_ba, w_aq, w_ak, w_av, w_gd, w_ga))
    lane_pad = lambda x: jnp.pad(x.reshape(1, -1), ((0, 0), (0, LANES - x.size)))
    return dict(
        norm_mix=norm_mix[0].reshape(1, -1), in_ws=in_ws,
        conv=dn_conv_w[0],
        alog=lane_pad(dn_a_log[0]), dtb=lane_pad(dn_dt_bias[0]),
        out_norm=dn_out_norm[0].reshape(1, -1), w_dn=bf(w_dn_branch[0]),
        q_norm=q_norm[0].reshape(1, -1), k_norm=k_norm[0].reshape(1, -1),
        w_at=bf(w_attn_branch[0]), w_o=bf(w_o[0]),
        norm_mlp=norm_mlp[0].reshape(1, -1), w_up=bf(w_up[0]), w_down=bf(w_down[0]),
        norm_ple=norm_ple[0].reshape(1, -1), w_pg=bf(w_ple_gate[0]), w_ple=bf(w_ple[0]))


def _tiles(seq):
    return dict(tm_proj=min(256, seq), cb=min(16, seq // DN_CHUNK),
                tq=min(256, seq), tk=min(128, seq), tm_merge=min(256, seq))


def _encoder_layer(x, pe, p):
    batch, seq, _ = x.shape
    assert seq % DN_CHUNK == 0 and seq % GRID_W == 0
    t = batch * seq
    tl = _tiles(seq)
    h = x.reshape(t, D_MODEL)
    cos, sin = _rope_tables(seq)
    qn, kn, vc, aux, z, aq, ak, av, gd, ga = _in_proj(
        h, p["norm_mix"], cos, sin, p["q_norm"], p["k_norm"], p["in_ws"], p["conv"], p["alog"], p["dtb"],
        seq=seq, tm=tl["tm_proj"])
    o = _delta_rule(qn, kn, vc, aux, batch=batch, seq=seq, cb=tl["cb"])
    att = _attention(aq, ak, av, batch=batch, seq=seq, tq=tl["tq"], tk=tl["tk"])
    h3 = _merge_mlp(h, o, z, att, gd, ga, pe.reshape(t, PLE_DIM), p["out_norm"], p["norm_mlp"], p["norm_ple"],
                    p["w_dn"], p["w_at"], p["w_o"], p["w_up"], p["w_down"], p["w_pg"], p["w_ple"],
                    tm=tl["tm_merge"])
    return h3.reshape(batch, seq, D_MODEL)


@jax.jit
def kernel(x_prompt, x_sample, p_prompt, p_sample, norm_mix, w_in, dn_conv_w, dn_a_log, dn_dt_bias,
           dn_out_norm, w_dn_branch, q_norm, k_norm, w_attn_branch, w_o, norm_mlp, w_up, w_down,
           norm_ple, w_ple_gate, w_ple):
    assert norm_mix.shape[0] == 1, "single-layer trunk"
    p = _prepare_weights(norm_mix, w_in, dn_conv_w, dn_a_log, dn_dt_bias, dn_out_norm, w_dn_branch,
                         q_norm, k_norm, w_attn_branch, w_o, norm_mlp, w_up, w_down, norm_ple,
                         w_ple_gate, w_ple)
    y_prompt = _encoder_layer(x_prompt, p_prompt[0], p)
    y_sample = _encoder_layer(x_sample, p_sample[0], p)
    return (y_prompt, y_sample)
```

```python
import functools
import math

import jax
import jax.numpy as jnp
from jax import lax
from jax.experimental import pallas as pl
from jax.experimental.pallas import tpu as pltpu

F32 = jnp.float32
BF16 = jnp.bfloat16

D_MODEL = 1024
PLE_DIM = 256
D_FF = 4 * D_MODEL
NORM_EPS = 1e-6
GRID_W = 64
DN_QK_HEADS = 4
DN_V_HEADS = 8
DN_DK = 128
DN_DV = 128
DN_CHUNK = 64
DN_CONV = 5
DN_QK_W = DN_QK_HEADS * DN_DK
DN_V_W = DN_V_HEADS * DN_DV
ATT_HEADS = 8
ATT_KV_HEADS = 2
ATT_GROUP = ATT_HEADS // ATT_KV_HEADS
HEAD_DIM = 128
ROPE_THETA = 10000.0
LOG2_E = math.log2(math.e)
ATT_Q_W = ATT_HEADS * HEAD_DIM
ATT_KV_W = ATT_KV_HEADS * HEAD_DIM
ATT_UNROLL = 32

LANES = 128
SUBLANES = 8
VMEM_LIMIT_BYTES = 56 * 1024 * 1024

DN_GROUP = 4
DN_GM = DN_GROUP * DN_CHUNK
DN_PIPE = 4
AUX_W = LANES
AUX_GC, AUX_BETA, AUX_TOT, AUX_EG = 0, 1, 2, 3


def _cparams(semantics):
    return pltpu.CompilerParams(dimension_semantics=semantics, vmem_limit_bytes=VMEM_LIMIT_BYTES)


def _resident(shape):
    nd = len(shape)
    return pl.BlockSpec(shape, lambda *_: (0,) * nd, pipeline_mode=pl.Buffered(1))


def _rms(x, gain):
    ms = jnp.mean(x * x, axis=-1, keepdims=True)
    return x * lax.rsqrt(ms + NORM_EPS) * gain


def _mm(a, b):
    return jnp.dot(a.astype(BF16), b.astype(BF16), preferred_element_type=F32)


def _mm_nt(a, b):
    return lax.dot_general(a.astype(BF16), b.astype(BF16), (((1,), (1,)), ((), ())),
                           preferred_element_type=F32)


def _mm_tn(a, b):
    return lax.dot_general(a.astype(BF16), b.astype(BF16), (((0,), (0,)), ((), ())),
                           preferred_element_type=F32)


def _sigmoid(x):
    return 1.0 / (1.0 + jnp.exp(-x))


def _silu(x):
    return x * _sigmoid(x)


def _swap32(x):
    lane = lax.broadcasted_iota(jnp.int32, x.shape, 1)
    first = (lane & 32) == 0
    return jnp.where(first, pltpu.roll(x, LANES - 32, 1), pltpu.roll(x, 32, 1))


def _norm_rope(x, gain, cos, sin, scale):
    y = _rms(x, gain)
    return (y * cos + _swap32(y) * sin) * scale


HALO = 16


def _conv_silu(ext_ref, w_ref, col0, width, tm):
    pad = DN_CONV // 2
    acc = None
    for tap in range(DN_CONV):
        start = HALO - pad + tap
        term = ext_ref[start:start + tm, col0:col0 + width] * w_ref[tap:tap + 1, col0:col0 + width]
        acc = term if acc is None else acc + term
    return _silu(acc)


def _split3(x):
    hi = x.astype(BF16)
    r1 = x - hi.astype(F32)
    mid = r1.astype(BF16)
    lo = (r1 - mid.astype(F32)).astype(BF16)
    return hi, mid, lo


def _ones_mm(mask_bf, x):
    hi, mid, lo = _split3(x)
    dot = lambda p: jnp.dot(mask_bf, p, preferred_element_type=F32)
    return dot(hi) + dot(mid) + dot(lo)


def _place(x, src, dst):
    shift = (dst - src) % LANES
    return x if shift == 0 else pltpu.roll(x, shift, 1)


def _gate_fields(ba, alog, dtb, aux_o, tm):
    beta = _sigmoid(ba[:, 0:LANES])
    a = ba[:, LANES:2 * LANES] + dtb
    softplus = jnp.maximum(a, 0.0) + jnp.log(1.0 + jnp.exp(-jnp.abs(a)))
    g = -jnp.exp(alog) * softplus

    ri = lax.broadcasted_iota(jnp.int32, (tm, tm), 0)
    ci = lax.broadcasted_iota(jnp.int32, (tm, tm), 1)
    same = lax.shift_right_logical(ri, 6) == lax.shift_right_logical(ci, 6)
    one = lambda m: jnp.where(m, 1.0, 0.0).astype(BF16)
    pre = _ones_mm(one(same & (ci <= ri)), g)
    suf = _ones_mm(one(same & (ci >= ri)), g)
    tot = _ones_mm(one(same), g)
    lane = lax.broadcasted_iota(jnp.int32, (tm, LANES), 1)
    gc = jnp.where(lane < DN_V_HEADS, pre, suf)
    eg = jnp.exp(tot)
    for d in range(2):
        src = d * DN_V_HEADS
        aux = jnp.zeros((tm, AUX_W), F32)
        for field, val in ((AUX_GC, gc), (AUX_BETA, beta), (AUX_TOT, tot), (AUX_EG, eg)):
            dst = field * DN_V_HEADS
            aux = jnp.where((lane >= dst) & (lane < dst + DN_V_HEADS), _place(val, src, dst), aux)
        aux_o[d] = aux


def _in_proj_kernel(x_ref, xp_ref, xn_ref, gain_ref, cos_ref, sin_ref, qn_ref, kn_ref,
                    wqkv_ref, wz_ref, wba_ref, waq_ref, wak_ref, wav_ref, wgd_ref, wga_ref,
                    conv_ref, alog_ref, dtb_ref,
                    q_o, k_o, v_o, aux_o, z_o, aq_o, ak_o, av_o, gd_o, ga_o, ext_ref, *, tm, nseq):
    i = pl.program_id(0)
    first = (i % nseq) == 0
    last = (i % nseq) == nseq - 1
    gain = gain_ref[...]
    norm = lambda ref: _rms(ref[...], gain).astype(BF16)
    u = norm(x_ref)

    def proj(w_ref):
        return jnp.dot(u, w_ref[...], preferred_element_type=F32)

    u_ext = jnp.concatenate([norm(xp_ref), u, norm(xn_ref)], axis=0)
    cos = cos_ref[...]
    sin = sin_ref[...]
    cbw = 2 * LANES

    def dn_block(b):
        cs = slice(b * cbw, (b + 1) * cbw)
        e = jnp.dot(u_ext, wqkv_ref[:, cs], preferred_element_type=F32)
        ext_ref[0:HALO, cs] = jnp.where(first, 0.0, e[0:HALO])
        ext_ref[HALO:HALO + tm, cs] = e[HALO:HALO + tm]
        ext_ref[HALO + tm:, cs] = jnp.where(last, 0.0, e[HALO + tm:])
        y = _conv_silu(ext_ref, conv_ref, b * cbw, cbw, tm)
        if b * cbw < 2 * DN_QK_W:
            scale = DN_DK ** -0.5 if b * cbw < DN_QK_W else 1.0
            out = q_o if b * cbw < DN_QK_W else k_o
            c0 = (b * cbw) % DN_QK_W
            for hh in range(cbw // DN_DK):
                yh = y[:, hh * DN_DK:(hh + 1) * DN_DK]
                r = lax.rsqrt(jnp.sum(yh * yh, axis=-1, keepdims=True) + NORM_EPS)
                out[:, c0 + hh * DN_DK:c0 + (hh + 1) * DN_DK] = (yh * (r * scale)).astype(out.dtype)
        else:
            c0 = b * cbw - 2 * DN_QK_W
            v_o[:, c0:c0 + cbw] = y.astype(v_o.dtype)

    def plain_block(w_ref, o_ref, b):
        cs = slice(b * cbw, (b + 1) * cbw)
        o_ref[:, cs] = jnp.dot(u, w_ref[:, cs], preferred_element_type=F32).astype(o_ref.dtype)

    def aq_block(b):
        cs = slice(b * cbw, (b + 1) * cbw)
        a = jnp.dot(u, waq_ref[:, cs], preferred_element_type=F32)
        for hh in range(cbw // HEAD_DIM):
            qh = _norm_rope(a[:, hh * HEAD_DIM:(hh + 1) * HEAD_DIM], qn_ref[...], cos, sin,
                            HEAD_DIM ** -0.5 * LOG2_E)
            r0 = b * cbw + hh * HEAD_DIM
            aq_o[r0:r0 + HEAD_DIM, :] = jnp.transpose(qh).astype(aq_o.dtype)

    def kv_block():
        ak = proj(wak_ref)
        av = proj(wav_ref)
        for h in range(ATT_KV_HEADS):
            sl = slice(h * HEAD_DIM, (h + 1) * HEAD_DIM)
            ak_o[:, sl] = _norm_rope(ak[:, sl], kn_ref[...], cos, sin, 1.0).astype(ak_o.dtype)
            av_o[sl, :] = jnp.transpose(av[:, sl]).astype(av_o.dtype)

    others = ([functools.partial(plain_block, wz_ref, z_o, b) for b in range(DN_V_W // cbw)]
              + [functools.partial(aq_block, b) for b in range(ATT_Q_W // cbw)]
              + [functools.partial(plain_block, wgd_ref, gd_o, b) for b in range(D_MODEL // cbw)]
              + [functools.partial(plain_block, wga_ref, ga_o, b) for b in range(D_MODEL // cbw)]
              + [kv_block, lambda: _gate_fields(proj(wba_ref), alog_ref[...], dtb_ref[...], aux_o, tm)])
    n_dn = (2 * DN_QK_W + DN_V_W) // cbw
    per = -(-len(others) // n_dn)
    for b in range(n_dn):
        dn_block(b)
        for f in others[b * per:(b + 1) * per]:
            f()


def _in_proj(x, gain, cos, sin, qn, kn, ws, conv, alog, dtb, *, seq, tm):
    t = x.shape[0]
    nseq = seq // tm
    hb = tm // HALO
    nhb = t // HALO
    row = lambda w: pl.BlockSpec((tm, w), lambda i: (i, 0))
    tab = pl.BlockSpec((tm, HEAD_DIM), lambda i: (i % nseq, 0))
    col = lambda w: pl.BlockSpec((w, tm), lambda i: (0, i))
    sds = jax.ShapeDtypeStruct
    outs = [(row(DN_QK_W), sds((t, DN_QK_W), BF16)), (row(DN_QK_W), sds((t, DN_QK_W), BF16)),
            (row(DN_V_W), sds((t, DN_V_W), BF16)),
            (pl.BlockSpec((2, tm, AUX_W), lambda i: (0, i, 0)), sds((2, t, AUX_W), F32)),
            (row(DN_V_W), sds((t, DN_V_W), BF16)), (col(ATT_Q_W), sds((ATT_Q_W, t), BF16)),
            (row(ATT_KV_W), sds((t, ATT_KV_W), BF16)), (col(ATT_KV_W), sds((ATT_KV_W, t), BF16)),
            (row(D_MODEL), sds((t, D_MODEL), BF16)), (row(D_MODEL), sds((t, D_MODEL), BF16))]
    return pl.pallas_call(
        functools.partial(_in_proj_kernel, tm=tm, nseq=nseq),
        grid=(t // tm,),
        in_specs=[row(D_MODEL),
                  pl.BlockSpec((HALO, D_MODEL), lambda i: (jnp.maximum(i * hb - 1, 0), 0)),
                  pl.BlockSpec((HALO, D_MODEL), lambda i: (jnp.minimum((i + 1) * hb, nhb - 1), 0)),
                  _resident((1, D_MODEL)), tab, tab, _resident((1, HEAD_DIM)), _resident((1, HEAD_DIM))]
        + [_resident(w.shape) for w in ws]
        + [_resident(conv.shape), _resident(alog.shape), _resident(dtb.shape)],
        out_specs=[o[0] for o in outs],
        out_shape=[o[1] for o in outs],
        scratch_shapes=[pltpu.VMEM((tm + 2 * HALO, 2 * DN_QK_W + DN_V_W), F32)],
        compiler_params=_cparams(("parallel",)),
        name="in_proj",
    )(x, x, x, gain, cos, sin, qn, kn, *ws, conv, alog, dtb)


def _unit_lower_inverses(mats, eye, m16, m32, expand):
    each = lambda f, *ls: [f(*xs) for xs in zip(*ls)]
    a32 = [jnp.where(m32 & ~m16, a, 0.0).astype(BF16) for a in mats]
    a64 = [jnp.where(m32, 0.0, a).astype(BF16) for a in mats]
    n = [jnp.where(m16, -a, 0.0) for a in mats]
    x = [eye + ni for ni in n]
    p = each(lambda ni: _mm(ni, expand(ni)), n)
    yield
    for level in range(3):
        pe = [expand(pi) for pi in p]
        x = each(lambda xi, pei: xi + _mm(xi, pei), x, pe)
        if level < 2:
            p = each(_mm, p, pe)
        yield
    for off in (a32, a64):
        t = each(lambda oi, xi: _mm(oi, expand(xi)), off, x)
        yield
        x = each(lambda xi, ti: xi - _mm(xi, expand(ti)), x, t)
        yield
    return x


def _interleave(*gens):
    live = list(gens)
    while live:
        for g in list(live):
            try:
                next(g)
            except StopIteration:
                live.remove(g)


def _delta_rule_kernel(q_ref, k_ref, v_ref, aux_ref, o_ref, s_ref, wq_ref, u_ref, kd_ref, aqk_ref, *, cb):
    d = pl.program_id(1)
    j = pl.program_id(2)

    @pl.when(j == 0)
    def _():
        s_ref[...] = jnp.zeros_like(s_ref)

    gm = DN_GM
    blk = lambda x, s: lax.shift_right_logical(x, s)
    bd = (blk(lax.broadcasted_iota(jnp.int32, (gm, gm), 0), 6)
          == blk(lax.broadcasted_iota(jnp.int32, (gm, gm), 1), 6))
    ri = lax.broadcasted_iota(jnp.int32, (DN_CHUNK, gm), 0)
    lane = lax.broadcasted_iota(jnp.int32, (DN_CHUNK, gm), 1)
    ci = lane & (DN_CHUNK - 1)
    lblock = blk(lane, 6)
    rel = (ri - ci) * (1 - 2 * d)
    incl = rel >= 0
    strict = rel > 0
    m16 = blk(ri, 4) == blk(ci, 4)
    m32 = blk(ri, 5) == blk(ci, 5)
    eye = jnp.where(ri == ci, 1.0, 0.0).astype(F32)

    def expand(c):
        return jnp.where(bd, jnp.concatenate([c.astype(F32)] * DN_GROUP, axis=0), 0.0).astype(BF16)

    ngroups = DN_V_HEADS // DN_GROUP

    def chunk_rows(jj):
        c = jnp.where(d == 0, jj, cb - 1 - jj)
        return pl.ds(pl.multiple_of(c * DN_CHUNK, DN_CHUNK), DN_CHUNK)

    f32 = lambda x: x.astype(F32)

    def local_work(jjs):
        chains = [(jj, hg) for jj in jjs for hg in range(ngroups)]
        amats, rhss = [], []
        for jj, hg in chains:
            rows = chunk_rows(jj)
            aux = aux_ref[0, rows, :]
            h0 = hg * DN_GROUP

            def stacked(field):
                base = field * DN_V_HEADS + h0
                return jnp.concatenate([jnp.broadcast_to(aux[:, base + r:base + r + 1], (DN_CHUNK, LANES))
                                        for r in range(DN_GROUP)], axis=0)

            def blocked(st):
                wide = [jnp.concatenate([st[r * DN_CHUNK:(r + 1) * DN_CHUNK]] * 2, axis=1)
                        for r in range(DN_GROUP)]
                return jnp.where(lblock < 2, jnp.where(lblock == 0, wide[0], wide[1]),
                                 jnp.where(lblock == 2, wide[2], wide[3]))

            gc_st, beta_st = stacked(AUX_GC), stacked(AUX_BETA)
            tot_st = jnp.concatenate(
                [jnp.broadcast_to(aux[0:1, AUX_TOT * DN_V_HEADS + h0 + r:AUX_TOT * DN_V_HEADS + h0 + r + 1],
                                  (DN_CHUNK, LANES)) for r in range(DN_GROUP)], axis=0)
            egc_st = jnp.exp(gc_st)
            egl_st = jnp.exp(tot_st - gc_st)
            gc_bl = blocked(gc_st)
            gc_row = jnp.sum(jnp.where(ri == ci, gc_bl, 0.0), axis=0, keepdims=True)

            qk_heads = [(h0 + r) // (DN_V_HEADS // DN_QK_HEADS) for r in range(DN_GROUP)]
            khead = lambda ref, h: ref[rows, h * DN_DK:(h + 1) * DN_DK]
            ks = jnp.concatenate([khead(k_ref, h) for h in qk_heads], axis=0)
            qs = jnp.concatenate([khead(q_ref, h) for h in qk_heads], axis=0)
            vs = jnp.concatenate([v_ref[rows, (h0 + r) * DN_DV:(h0 + r + 1) * DN_DV]
                                  for r in range(DN_GROUP)], axis=0)

            first = lblock < 2
            gram = jnp.where(first, _mm_nt(khead(k_ref, qk_heads[0]), ks), _mm_nt(khead(k_ref, qk_heads[2]), ks))
            qk = jnp.where(first, _mm_nt(khead(q_ref, qk_heads[0]), ks), _mm_nt(khead(q_ref, qk_heads[2]), ks))
            decay = jnp.where(incl, jnp.exp(jnp.where(incl, gc_bl - gc_row, 0.0)), 0.0)
            amats.append(jnp.where(strict, blocked(beta_st) * gram * decay, 0.0))
            aqk_ref[jj, hg] = expand(qk * decay)
            rhss.append(jnp.concatenate([f32(vs) * beta_st, f32(ks) * (beta_st * egc_st)],
                                        axis=1).astype(BF16))
            kd_ref[jj, hg] = (f32(ks) * egl_st).astype(BF16)
            qd = (f32(qs) * egc_st).astype(BF16)
            for r in range(DN_GROUP):
                wq_ref[jj, h0 + r, DN_CHUNK:2 * DN_CHUNK, :] = qd[r * DN_CHUNK:(r + 1) * DN_CHUNK]
        yield
        tinvs = yield from _unit_lower_inverses(amats, eye, m16, m32, expand)
        for (jj, hg), tinv, rhs in zip(chains, tinvs, rhss):
            sol = jnp.dot(expand(tinv), rhs, preferred_element_type=F32)
            u_ref[jj, hg] = sol[:, 0:DN_DV]
            w = sol[:, DN_DV:DN_DV + DN_DK].astype(BF16)
            for r in range(DN_GROUP):
                wq_ref[jj, hg * DN_GROUP + r, 0:DN_CHUNK, :] = w[r * DN_CHUNK:(r + 1) * DN_CHUNK]
        yield

    def recurrence(jjs):
        hslice = lambda h: slice(h * DN_DV, (h + 1) * DN_DV)
        for jj in jjs:
            rows = chunk_rows(jj)
            eg_row = aux_ref[0, pl.ds(rows.start, 1), :]
            states = [s_ref[:, hslice(h)] for h in range(DN_V_HEADS)]
            res = [jnp.dot(wq_ref[jj, h], states[h].astype(BF16), preferred_element_type=F32)
                   for h in range(DN_V_HEADS)]
            yield
            vnew = []
            for h in range(DN_V_HEADS):
                hg, r = divmod(h, DN_GROUP)
                vnew.append(u_ref[jj, hg, r * DN_CHUNK:(r + 1) * DN_CHUNK, :] - res[h][0:DN_CHUNK])
            for h in range(DN_V_HEADS):
                hg, r = divmod(h, DN_GROUP)
                eg = eg_row[:, AUX_EG * DN_V_HEADS + h:AUX_EG * DN_V_HEADS + h + 1]
                kd = kd_ref[jj, hg, r * DN_CHUNK:(r + 1) * DN_CHUNK, :]
                s_ref[:, hslice(h)] = states[h] * eg + _mm_tn(kd, vnew[h])
            yield
            for hg in range(ngroups):
                hs = range(hg * DN_GROUP, (hg + 1) * DN_GROUP)
                out = (jnp.concatenate([res[h][DN_CHUNK:2 * DN_CHUNK] for h in hs], axis=0)
                       + _mm(aqk_ref[jj, hg], jnp.concatenate([vnew[h] for h in hs], axis=0)))
                for r, h in enumerate(hs):
                    o_ref[0, rows, hslice(h)] = out[r * DN_CHUNK:(r + 1) * DN_CHUNK].astype(o_ref.dtype)
            yield

    stages = [range(i, min(i + DN_PIPE, cb)) for i in range(0, cb, DN_PIPE)]
    _interleave(local_work(stages[0]))
    for prev, cur in zip(stages, stages[1:]):
        _interleave(local_work(cur), recurrence(prev))
    _interleave(recurrence(stages[-1]))


def _delta_rule(q, k, v, aux, *, batch, seq, cb):
    t = q.shape[0]
    rb = cb * DN_CHUNK
    nb = seq // rb
    ngroups = DN_V_HEADS // DN_GROUP

    def rblock(b, d, j):
        return b * nb + jnp.where(d == 0, j, nb - 1 - j)

    return pl.pallas_call(
        functools.partial(_delta_rule_kernel, cb=cb),
        grid=(batch, 2, nb),
        in_specs=[pl.BlockSpec((rb, DN_QK_W), lambda b, d, j: (rblock(b, d, j), 0)),
                  pl.BlockSpec((rb, DN_QK_W), lambda b, d, j: (rblock(b, d, j), 0)),
                  pl.BlockSpec((rb, DN_V_W), lambda b, d, j: (rblock(b, d, j), 0)),
                  pl.BlockSpec((1, rb, AUX_W), lambda b, d, j: (d, rblock(b, d, j), 0))],
        out_specs=pl.BlockSpec((1, rb, DN_V_W), lambda b, d, j: (d, rblock(b, d, j), 0)),
        out_shape=jax.ShapeDtypeStruct((2, t, DN_V_W), BF16),
        scratch_shapes=[pltpu.VMEM((DN_DK, DN_V_HEADS * DN_DV), F32),
                        pltpu.VMEM((cb, DN_V_HEADS, 2 * DN_CHUNK, DN_DK), BF16),
                        pltpu.VMEM((cb, ngroups, DN_GM, DN_DV), F32),
                        pltpu.VMEM((cb, ngroups, DN_GM, DN_DK), BF16),
                        pltpu.VMEM((cb, ngroups, DN_GM, DN_GM), BF16)],
        compiler_params=_cparams(("parallel", "parallel", "arbitrary")),
        name="delta_rule",
    )(q, k, v, aux)


def _attention_kernel(qt_ref, k_ref, vt_ref, o_ref, acc_ref, m_ref, l_ref, *, tk, seq):
    m_ref[...] = jnp.full(m_ref.shape, -jnp.inf, F32)
    l_ref[...] = jnp.zeros(l_ref.shape, F32)
    acc_ref[...] = jnp.zeros(acc_ref.shape, F32)

    def step(c, carry):
        ks = pl.ds(pl.multiple_of(c * tk, tk), tk)
        kc = k_ref[ks, :]
        vtc = vt_ref[:, ks]
        for g in range(ATT_GROUP):
            s = jnp.dot(kc, qt_ref[g * HEAD_DIM:(g + 1) * HEAD_DIM, :], preferred_element_type=F32)
            m_old = m_ref[g:g + 1, :]
            m_new = jnp.maximum(m_old, jnp.max(s, axis=0, keepdims=True))
            alpha = jnp.exp2(m_old - m_new)
            p = jnp.exp2(s - m_new)
            l_ref[g:g + 1, :] = alpha * l_ref[g:g + 1, :] + jnp.sum(p, axis=0, keepdims=True)
            m_ref[g:g + 1, :] = m_new
            acc_ref[g] = alpha * acc_ref[g] + jnp.dot(vtc, p.astype(BF16), preferred_element_type=F32)
        return carry

    nsteps = seq // tk
    lax.fori_loop(0, nsteps, step, 0, unroll=math.gcd(nsteps, ATT_UNROLL))
    for g in range(ATT_GROUP):
        out = acc_ref[g] / l_ref[g:g + 1, :]
        o_ref[:, g * HEAD_DIM:(g + 1) * HEAD_DIM] = jnp.transpose(out).astype(o_ref.dtype)


def _attention(aqt, ak, avt, *, batch, seq, tq, tk):
    t = ak.shape[0]
    nq = seq // tq
    gw = ATT_GROUP * HEAD_DIM
    return pl.pallas_call(
        functools.partial(_attention_kernel, tk=tk, seq=seq),
        grid=(batch, ATT_KV_HEADS, nq),
        in_specs=[pl.BlockSpec((gw, tq), lambda b, g, i: (g, b * nq + i)),
                  pl.BlockSpec((seq, HEAD_DIM), lambda b, g, i: (b, g)),
                  pl.BlockSpec((HEAD_DIM, seq), lambda b, g, i: (g, b))],
        out_specs=pl.BlockSpec((tq, gw), lambda b, g, i: (b * nq + i, g)),
        out_shape=jax.ShapeDtypeStruct((t, ATT_Q_W), BF16),
        scratch_shapes=[pltpu.VMEM((ATT_GROUP, HEAD_DIM, tq), F32), pltpu.VMEM((SUBLANES, tq), F32),
                        pltpu.VMEM((SUBLANES, tq), F32)],
        compiler_params=_cparams(("parallel", "parallel", "parallel")),
        name="attention",
    )(aqt, ak, avt)


def _merge_mlp_kernel(h_ref, o_ref, z_ref, att_ref, gd_ref, ga_ref, pe_ref, on_ref, nm_ref, np_ref,
                      wdn_ref, wat_ref, wo_ref, wup_ref, wdown_ref, wpg_ref, wple_ref, out_ref):
    dot = lambda a, w_ref: jnp.dot(a, w_ref[...], preferred_element_type=F32)
    o = o_ref[0].astype(F32) + o_ref[1].astype(F32)
    z = z_ref[...].astype(F32)
    parts = []
    for h in range(DN_V_HEADS):
        sl = slice(h * DN_DV, (h + 1) * DN_DV)
        parts.append((_rms(o[:, sl], on_ref[...]) * _silu(z[:, sl])).astype(BF16))
    y_dn = dot(jnp.concatenate(parts, axis=1), wdn_ref)
    y_at = dot(att_ref[...], wat_ref)
    merged = _sigmoid(gd_ref[...].astype(F32)) * y_dn + _sigmoid(ga_ref[...].astype(F32)) * y_at
    h1 = h_ref[...] + dot(merged.astype(BF16), wo_ref)
    hm = dot(_rms(h1, nm_ref[...]).astype(BF16), wup_ref)
    h2 = h1 + dot(jnp.square(jnp.maximum(hm, 0.0)).astype(BF16), wdown_ref)
    gate = _sigmoid(dot(_rms(h2, np_ref[...]).astype(BF16), wpg_ref))
    out_ref[...] = h2 + gate * dot(pe_ref[...].astype(BF16), wple_ref)


def _merge_mlp(h, o, z, att, gd, ga, pe, on, nm, npl, wdn, wat, wo, wup, wdown, wpg, wple, *, tm):
    t = h.shape[0]
    row = lambda w: pl.BlockSpec((tm, w), lambda i: (i, 0))
    consts = (on, nm, npl, wdn, wat, wo, wup, wdown, wpg, wple)
    return pl.pallas_call(
        _merge_mlp_kernel,
        grid=(t // tm,),
        in_specs=[row(D_MODEL), pl.BlockSpec((2, tm, DN_V_W), lambda i: (0, i, 0)), row(DN_V_W),
                  row(ATT_Q_W), row(D_MODEL), row(D_MODEL), row(PLE_DIM)]
        + [_resident(c.shape) for c in consts],
        out_specs=row(D_MODEL),
        out_shape=jax.ShapeDtypeStruct((t, D_MODEL), F32),
        compiler_params=_cparams(("parallel",)),
        name="merge_mlp",
    )(h, o, z, att, gd, ga, pe, *consts)


def _rope_tables(seq):
    pos = jnp.arange(seq)
    r = (pos // GRID_W).astype(F32)
    c = (pos % GRID_W).astype(F32)
    half = HEAD_DIM // 2
    inv = jnp.power(jnp.float32(ROPE_THETA), -jnp.arange(0, half, 2, dtype=F32) / half)
    ar = r[:, None] * inv
    ac = c[:, None] * inv
    cos = jnp.concatenate([jnp.cos(ar), jnp.cos(ar), jnp.cos(ac), jnp.cos(ac)], axis=1)
    sin = jnp.concatenate([-jnp.sin(ar), jnp.sin(ar), -jnp.sin(ac), jnp.sin(ac)], axis=1)
    return cos, sin


def _prepare_weights(norm_mix, w_in, dn_conv_w, dn_a_log, dn_dt_bias, dn_out_norm, w_dn_branch,
                     q_norm, k_norm, w_attn_branch, w_o, norm_mlp, w_up, w_down, norm_ple,
                     w_ple_gate, w_ple):
    bf = lambda w: w.astype(BF16)
    w_in = w_in[0]
    splits = (2 * DN_QK_W + DN_V_W, DN_V_W, 2 * DN_V_HEADS, 2 * DN_V_HEADS, ATT_Q_W, ATT_KV_W, ATT_KV_W,
              D_MODEL, D_MODEL)
    cols, start = [], 0
    for s in splits:
        cols.append(w_in[:, start:start + s])
        start += s
    w_qkv, w_z, w_b, w_a, w_aq, w_ak, w_av, w_gd, w_ga = cols
    pad = jnp.zeros((D_MODEL, LANES - 2 * DN_V_HEADS), F32)
    w_ba = jnp.concatenate([w_b, pad, w_a, pad], axis=1)
    in_ws = tuple(bf(w) for w in (w_qkv, w_z, w_ba, w_aq, w_ak, w_av, w_gd, w_ga))
    lane_pad = lambda x: jnp.pad(x.reshape(1, -1), ((0, 0), (0, LANES - x.size)))
    return dict(
        norm_mix=norm_mix[0].reshape(1, -1), in_ws=in_ws,
        conv=dn_conv_w[0],
        alog=lane_pad(dn_a_log[0]), dtb=lane_pad(dn_dt_bias[0]),
        out_norm=dn_out_norm[0].reshape(1, -1), w_dn=bf(w_dn_branch[0]),
        q_norm=q_norm[0].reshape(1, -1), k_norm=k_norm[0].reshape(1, -1),
        w_at=bf(w_attn_branch[0]), w_o=bf(w_o[0]),
        norm_mlp=norm_mlp[0].reshape(1, -1), w_up=bf(w_up[0]), w_down=bf(w_down[0]),
        norm_ple=norm_ple[0].reshape(1, -1), w_pg=bf(w_ple_gate[0]), w_ple=bf(w_ple[0]))


def _tiles(seq):
    return dict(tm_proj=min(256, seq), cb=min(16, seq // DN_CHUNK),
                tq=min(256, seq), tk=min(128, seq), tm_merge=min(256, seq))


def _encoder_layer(x, pe, p):
    batch, seq, _ = x.shape
    assert seq % DN_CHUNK == 0 and seq % GRID_W == 0
    t = batch * seq
    tl = _tiles(seq)
    h = x.reshape(t, D_MODEL)
    cos, sin = _rope_tables(seq)
    qn, kn, vc, aux, z, aq, ak, av, gd, ga = _in_proj(
        h, p["norm_mix"], cos, sin, p["q_norm"], p["k_norm"], p["in_ws"], p["conv"], p["alog"], p["dtb"],
        seq=seq, tm=tl["tm_proj"])
    o = _delta_rule(qn, kn, vc, aux, batch=batch, seq=seq, cb=tl["cb"])
    att = _attention(aq, ak, av, batch=batch, seq=seq, tq=tl["tq"], tk=tl["tk"])
    h3 = _merge_mlp(h, o, z, att, gd, ga, pe.reshape(t, PLE_DIM), p["out_norm"], p["norm_mlp"], p["norm_ple"],
                    p["w_dn"], p["w_at"], p["w_o"], p["w_up"], p["w_down"], p["w_pg"], p["w_ple"],
                    tm=tl["tm_merge"])
    return h3.reshape(batch, seq, D_MODEL)


@jax.jit
def kernel(x_prompt, x_sample, p_prompt, p_sample, norm_mix, w_in, dn_conv_w, dn_a_log, dn_dt_bias,
           dn_out_norm, w_dn_branch, q_norm, k_norm, w_attn_branch, w_o, norm_mlp, w_up, w_down,
           norm_ple, w_ple_gate, w_ple):
    assert norm_mix.shape[0] == 1, "single-layer trunk"
    p = _prepare_weights(norm_mix, w_in, dn_conv_w, dn_a_log, dn_dt_bias, dn_out_norm, w_dn_branch,
                         q_norm, k_norm, w_attn_branch, w_o, norm_mlp, w_up, w_down, norm_ple,
                         w_ple_gate, w_ple)
    y_prompt = _encoder_layer(x_prompt, p_prompt[0], p)
    y_sample = _encoder_layer(x_sample, p_sample[0], p)
    return (y_prompt, y_sample)
```

```python
import functools
import math

import jax
import jax.numpy as jnp
import numpy as np
from jax import lax
from jax.experimental import pallas as pl
from jax.experimental.pallas import tpu as pltpu

F32 = jnp.float32
BF16 = jnp.bfloat16

D_MODEL = 1024
PLE_DIM = 256
D_FF = 4 * D_MODEL
NORM_EPS = 1e-6
GRID_W = 64
DN_QK_HEADS = 4
DN_V_HEADS = 8
DN_DK = 128
DN_DV = 128
DN_CHUNK = 64
DN_CONV = 5
DN_QK_W = DN_QK_HEADS * DN_DK
DN_V_W = DN_V_HEADS * DN_DV
ATT_HEADS = 8
ATT_KV_HEADS = 2
ATT_GROUP = ATT_HEADS // ATT_KV_HEADS
HEAD_DIM = 128
ROPE_THETA = 10000.0
LOG2_E = math.log2(math.e)
ATT_Q_W = ATT_HEADS * HEAD_DIM
ATT_KV_W = ATT_KV_HEADS * HEAD_DIM
ATT_UNROLL = 32

LANES = 128
SUBLANES = 8
VMEM_LIMIT_BYTES = 56 * 1024 * 1024

DN_GROUP = 4
DN_GM = DN_GROUP * DN_CHUNK
DN_PIPE = 4
AUX_W = LANES
AUX_GC, AUX_BETA, AUX_TOT, AUX_EG = 0, 1, 2, 3


def _cparams(semantics):
    return pltpu.CompilerParams(dimension_semantics=semantics, vmem_limit_bytes=VMEM_LIMIT_BYTES)


def _resident(shape):
    nd = len(shape)
    return pl.BlockSpec(shape, lambda *_: (0,) * nd, pipeline_mode=pl.Buffered(1))


def _rms(x, gain):
    ms = jnp.mean(x * x, axis=-1, keepdims=True)
    return x * lax.rsqrt(ms + NORM_EPS) * gain


def _mm(a, b):
    return jnp.dot(a.astype(BF16), b.astype(BF16), preferred_element_type=F32)


def _mm_nt(a, b):
    return lax.dot_general(a.astype(BF16), b.astype(BF16), (((1,), (1,)), ((), ())),
                           preferred_element_type=F32)


def _mm_tn(a, b):
    return lax.dot_general(a.astype(BF16), b.astype(BF16), (((0,), (0,)), ((), ())),
                           preferred_element_type=F32)


def _sigmoid(x):
    return 1.0 / (1.0 + jnp.exp(-x))


def _silu(x):
    return x * _sigmoid(x)


def _swap32(x):
    lane = lax.broadcasted_iota(jnp.int32, x.shape, 1)
    first = (lane & 32) == 0
    return jnp.where(first, pltpu.roll(x, LANES - 32, 1), pltpu.roll(x, 32, 1))


def _norm_rope(x, gain, cos, sin, scale):
    y = _rms(x, gain)
    return (y * cos + _swap32(y) * sin) * scale


HALO = 16


def _conv_silu(ext_ref, w_ref, col0, width, tm):
    pad = DN_CONV // 2
    acc = None
    for tap in range(DN_CONV):
        start = HALO - pad + tap
        term = ext_ref[start:start + tm, col0:col0 + width] * w_ref[tap:tap + 1, col0:col0 + width]
        acc = term if acc is None else acc + term
    return _silu(acc)


def _split3(x):
    hi = x.astype(BF16)
    r1 = x - hi.astype(F32)
    mid = r1.astype(BF16)
    lo = (r1 - mid.astype(F32)).astype(BF16)
    return hi, mid, lo


def _ones_mm(mask_bf, x):
    hi, mid, lo = _split3(x)
    dot = lambda p: jnp.dot(mask_bf, p, preferred_element_type=F32)
    return dot(hi) + dot(mid) + dot(lo)


def _place(x, src, dst):
    shift = (dst - src) % LANES
    return x if shift == 0 else pltpu.roll(x, shift, 1)


def _gate_fields(ba, alog, dtb, aux_o, tm):
    beta = _sigmoid(ba[:, 0:LANES])
    a = ba[:, LANES:2 * LANES] + dtb
    softplus = jnp.maximum(a, 0.0) + jnp.log(1.0 + jnp.exp(-jnp.abs(a)))
    g = -jnp.exp(alog) * softplus

    ri = lax.broadcasted_iota(jnp.int32, (tm, tm), 0)
    ci = lax.broadcasted_iota(jnp.int32, (tm, tm), 1)
    same = lax.shift_right_logical(ri, 6) == lax.shift_right_logical(ci, 6)
    one = lambda m: jnp.where(m, 1.0, 0.0).astype(BF16)
    pre = _ones_mm(one(same & (ci <= ri)), g)
    suf = _ones_mm(one(same & (ci >= ri)), g)
    tot = _ones_mm(one(same), g)
    lane = lax.broadcasted_iota(jnp.int32, (tm, LANES), 1)
    gc = jnp.where(lane < DN_V_HEADS, pre, suf)
    eg = jnp.exp(tot)
    for d in range(2):
        src = d * DN_V_HEADS
        aux = jnp.zeros((tm, AUX_W), F32)
        for field, val in ((AUX_GC, gc), (AUX_BETA, beta), (AUX_TOT, tot), (AUX_EG, eg)):
            dst = field * DN_V_HEADS
            aux = jnp.where((lane >= dst) & (lane < dst + DN_V_HEADS), _place(val, src, dst), aux)
        aux_o[d] = aux


def _in_proj_kernel(x_ref, xp_ref, xn_ref, gain_ref, cos_ref, sin_ref, qn_ref, kn_ref,
                    wqkv_ref, wz_ref, wba_ref, waq_ref, wak_ref, wav_ref, wgd_ref, wga_ref,
                    conv_ref, alog_ref, dtb_ref,
                    q_o, k_o, v_o, aux_o, z_o, aq_o, ak_o, av_o, gd_o, ga_o, ext_ref, *, tm, nseq):
    i = pl.program_id(0)
    first = (i % nseq) == 0
    last = (i % nseq) == nseq - 1
    gain = gain_ref[...]
    norm = lambda ref: _rms(ref[...], gain).astype(BF16)
    u = norm(x_ref)

    def proj(w_ref):
        return jnp.dot(u, w_ref[...], preferred_element_type=F32)

    u_ext = jnp.concatenate([norm(xp_ref), u, norm(xn_ref)], axis=0)
    cos = cos_ref[...]
    sin = sin_ref[...]
    cbw = 2 * LANES

    def dn_block(b):
        cs = slice(b * cbw, (b + 1) * cbw)
        e = jnp.dot(u_ext, wqkv_ref[:, cs], preferred_element_type=F32)
        ext_ref[0:HALO, cs] = jnp.where(first, 0.0, e[0:HALO])
        ext_ref[HALO:HALO + tm, cs] = e[HALO:HALO + tm]
        ext_ref[HALO + tm:, cs] = jnp.where(last, 0.0, e[HALO + tm:])
        y = _conv_silu(ext_ref, conv_ref, b * cbw, cbw, tm)
        if b * cbw < 2 * DN_QK_W:
            scale = DN_DK ** -0.5 if b * cbw < DN_QK_W else 1.0
            out = q_o if b * cbw < DN_QK_W else k_o
            c0 = (b * cbw) % DN_QK_W
            for hh in range(cbw // DN_DK):
                yh = y[:, hh * DN_DK:(hh + 1) * DN_DK]
                r = lax.rsqrt(jnp.sum(yh * yh, axis=-1, keepdims=True) + NORM_EPS)
                out[:, c0 + hh * DN_DK:c0 + (hh + 1) * DN_DK] = (yh * (r * scale)).astype(out.dtype)
        else:
            c0 = b * cbw - 2 * DN_QK_W
            v_o[:, c0:c0 + cbw] = y.astype(v_o.dtype)

    def plain_block(w_ref, o_ref, b):
        cs = slice(b * cbw, (b + 1) * cbw)
        o_ref[:, cs] = jnp.dot(u, w_ref[:, cs], preferred_element_type=F32).astype(o_ref.dtype)

    def aq_block(b):
        cs = slice(b * cbw, (b + 1) * cbw)
        a = jnp.dot(u, waq_ref[:, cs], preferred_element_type=F32)
        for hh in range(cbw // HEAD_DIM):
            qh = _norm_rope(a[:, hh * HEAD_DIM:(hh + 1) * HEAD_DIM], qn_ref[...], cos, sin,
                            HEAD_DIM ** -0.5 * LOG2_E)
            r0 = b * cbw + hh * HEAD_DIM
            aq_o[r0:r0 + HEAD_DIM, :] = jnp.transpose(qh).astype(aq_o.dtype)

    def kv_block():
        ak = proj(wak_ref)
        av = proj(wav_ref)
        for h in range(ATT_KV_HEADS):
            sl = slice(h * HEAD_DIM, (h + 1) * HEAD_DIM)
            ak_o[:, sl] = _norm_rope(ak[:, sl], kn_ref[...], cos, sin, 1.0).astype(ak_o.dtype)
            av_o[sl, :] = jnp.transpose(av[:, sl]).astype(av_o.dtype)

    others = ([functools.partial(plain_block, wz_ref, z_o, b) for b in range(DN_V_W // cbw)]
              + [functools.partial(aq_block, b) for b in range(ATT_Q_W // cbw)]
              + [functools.partial(plain_block, wgd_ref, gd_o, b) for b in range(D_MODEL // cbw)]
              + [functools.partial(plain_block, wga_ref, ga_o, b) for b in range(D_MODEL // cbw)]
              + [kv_block, lambda: _gate_fields(proj(wba_ref), alog_ref[...], dtb_ref[...], aux_o, tm)])
    n_dn = (2 * DN_QK_W + DN_V_W) // cbw
    per = -(-len(others) // n_dn)
    for b in range(n_dn):
        dn_block(b)
        for f in others[b * per:(b + 1) * per]:
            f()


def _in_proj(x, gain, cos, sin, qn, kn, ws, conv, alog, dtb, *, seq, tm):
    t = x.shape[0]
    nseq = seq // tm
    hb = tm // HALO
    nhb = t // HALO
    row = lambda w: pl.BlockSpec((tm, w), lambda i: (i, 0))
    tab = pl.BlockSpec((tm, HEAD_DIM), lambda i: (i % nseq, 0))
    col = lambda w: pl.BlockSpec((w, tm), lambda i: (0, i))
    sds = jax.ShapeDtypeStruct
    outs = [(row(DN_QK_W), sds((t, DN_QK_W), BF16)), (row(DN_QK_W), sds((t, DN_QK_W), BF16)),
            (row(DN_V_W), sds((t, DN_V_W), BF16)),
            (pl.BlockSpec((2, tm, AUX_W), lambda i: (0, i, 0)), sds((2, t, AUX_W), F32)),
            (row(DN_V_W), sds((t, DN_V_W), BF16)), (col(ATT_Q_W), sds((ATT_Q_W, t), BF16)),
            (row(ATT_KV_W), sds((t, ATT_KV_W), BF16)), (col(ATT_KV_W), sds((ATT_KV_W, t), BF16)),
            (row(D_MODEL), sds((t, D_MODEL), BF16)), (row(D_MODEL), sds((t, D_MODEL), BF16))]
    return pl.pallas_call(
        functools.partial(_in_proj_kernel, tm=tm, nseq=nseq),
        grid=(t // tm,),
        in_specs=[row(D_MODEL),
                  pl.BlockSpec((HALO, D_MODEL), lambda i: (jnp.maximum(i * hb - 1, 0), 0)),
                  pl.BlockSpec((HALO, D_MODEL), lambda i: (jnp.minimum((i + 1) * hb, nhb - 1), 0)),
                  _resident((1, D_MODEL)), tab, tab, _resident((1, HEAD_DIM)), _resident((1, HEAD_DIM))]
        + [_resident(w.shape) for w in ws]
        + [_resident(conv.shape), _resident(alog.shape), _resident(dtb.shape)],
        out_specs=[o[0] for o in outs],
        out_shape=[o[1] for o in outs],
        scratch_shapes=[pltpu.VMEM((tm + 2 * HALO, 2 * DN_QK_W + DN_V_W), F32)],
        compiler_params=_cparams(("parallel",)),
        name="in_proj",
    )(x, x, x, gain, cos, sin, qn, kn, *ws, conv, alog, dtb)


def _unit_lower_inverses(mats, eye, m16, m32, expand):
    each = lambda f, *ls: [f(*xs) for xs in zip(*ls)]
    a32 = [jnp.where(m32 & ~m16, a, 0.0).astype(BF16) for a in mats]
    a64 = [jnp.where(m32, 0.0, a).astype(BF16) for a in mats]
    n = [jnp.where(m16, -a, 0.0) for a in mats]
    x = [eye + ni for ni in n]
    p = each(lambda ni: _mm(ni, expand(ni)), n)
    yield
    for level in range(3):
        pe = [expand(pi) for pi in p]
        x = each(lambda xi, pei: xi + _mm(xi, pei), x, pe)
        if level < 2:
            p = each(_mm, p, pe)
        yield
    for off in (a32, a64):
        t = each(lambda oi, xi: _mm(oi, expand(xi)), off, x)
        yield
        x = each(lambda xi, ti: xi - _mm(xi, expand(ti)), x, t)
        yield
    return x


def _interleave(*gens):
    live = list(gens)
    while live:
        for g in list(live):
            try:
                next(g)
            except StopIteration:
                live.remove(g)


def _delta_rule_kernel(q_ref, k_ref, v_ref, aux_ref, o_ref, s_ref, wq_ref, u_ref, kd_ref, aqk_ref, *, cb):
    d = pl.program_id(1)
    j = pl.program_id(2)

    @pl.when(j == 0)
    def _():
        s_ref[...] = jnp.zeros_like(s_ref)

    gm = DN_GM
    blk = lambda x, s: lax.shift_right_logical(x, s)
    bd = (blk(lax.broadcasted_iota(jnp.int32, (gm, gm), 0), 6)
          == blk(lax.broadcasted_iota(jnp.int32, (gm, gm), 1), 6))
    ri = lax.broadcasted_iota(jnp.int32, (DN_CHUNK, gm), 0)
    lane = lax.broadcasted_iota(jnp.int32, (DN_CHUNK, gm), 1)
    ci = lane & (DN_CHUNK - 1)
    lblock = blk(lane, 6)
    rel = (ri - ci) * (1 - 2 * d)
    incl = rel >= 0
    strict = rel > 0
    m16 = blk(ri, 4) == blk(ci, 4)
    m32 = blk(ri, 5) == blk(ci, 5)
    eye = jnp.where(ri == ci, 1.0, 0.0).astype(F32)

    def expand(c):
        return jnp.where(bd, jnp.concatenate([c.astype(F32)] * DN_GROUP, axis=0), 0.0).astype(BF16)

    ngroups = DN_V_HEADS // DN_GROUP

    def chunk_rows(jj):
        c = jnp.where(d == 0, jj, cb - 1 - jj)
        return pl.ds(pl.multiple_of(c * DN_CHUNK, DN_CHUNK), DN_CHUNK)

    f32 = lambda x: x.astype(F32)

    def local_work(jjs):
        chains = [(jj, hg) for jj in jjs for hg in range(ngroups)]
        amats, rhss = [], []
        for jj, hg in chains:
            rows = chunk_rows(jj)
            aux = aux_ref[0, rows, :]
            h0 = hg * DN_GROUP

            def stacked(field):
                base = field * DN_V_HEADS + h0
                return jnp.concatenate([jnp.broadcast_to(aux[:, base + r:base + r + 1], (DN_CHUNK, LANES))
                                        for r in range(DN_GROUP)], axis=0)

            def blocked(st):
                wide = [jnp.concatenate([st[r * DN_CHUNK:(r + 1) * DN_CHUNK]] * 2, axis=1)
                        for r in range(DN_GROUP)]
                return jnp.where(lblock < 2, jnp.where(lblock == 0, wide[0], wide[1]),
                                 jnp.where(lblock == 2, wide[2], wide[3]))

            gc_st, beta_st = stacked(AUX_GC), stacked(AUX_BETA)
            tot_st = jnp.concatenate(
                [jnp.broadcast_to(aux[0:1, AUX_TOT * DN_V_HEADS + h0 + r:AUX_TOT * DN_V_HEADS + h0 + r + 1],
                                  (DN_CHUNK, LANES)) for r in range(DN_GROUP)], axis=0)
            egc_st = jnp.exp(gc_st)
            egl_st = jnp.exp(tot_st - gc_st)
            gc_bl = blocked(gc_st)
            gc_row = jnp.sum(jnp.where(ri == ci, gc_bl, 0.0), axis=0, keepdims=True)

            qk_heads = [(h0 + r) // (DN_V_HEADS // DN_QK_HEADS) for r in range(DN_GROUP)]
            khead = lambda ref, h: ref[rows, h * DN_DK:(h + 1) * DN_DK]
            ks = jnp.concatenate([khead(k_ref, h) for h in qk_heads], axis=0)
            qs = jnp.concatenate([khead(q_ref, h) for h in qk_heads], axis=0)
            vs = jnp.concatenate([v_ref[rows, (h0 + r) * DN_DV:(h0 + r + 1) * DN_DV]
                                  for r in range(DN_GROUP)], axis=0)

            first = lblock < 2
            gram = jnp.where(first, _mm_nt(khead(k_ref, qk_heads[0]), ks), _mm_nt(khead(k_ref, qk_heads[2]), ks))
            qk = jnp.where(first, _mm_nt(khead(q_ref, qk_heads[0]), ks), _mm_nt(khead(q_ref, qk_heads[2]), ks))
            decay = jnp.where(incl, jnp.exp(jnp.where(incl, gc_bl - gc_row, 0.0)), 0.0)
            amats.append(jnp.where(strict, blocked(beta_st) * gram * decay, 0.0))
            aqk_ref[jj, hg] = expand(qk * decay)
            rhss.append(jnp.concatenate([f32(vs) * beta_st, f32(ks) * (beta_st * egc_st)],
                                        axis=1).astype(BF16))
            kd_ref[jj, hg] = (f32(ks) * egl_st).astype(BF16)
            qd = (f32(qs) * egc_st).astype(BF16)
            for r in range(DN_GROUP):
                wq_ref[jj, h0 + r, DN_CHUNK:2 * DN_CHUNK, :] = qd[r * DN_CHUNK:(r + 1) * DN_CHUNK]
        yield
        tinvs = yield from _unit_lower_inverses(amats, eye, m16, m32, expand)
        for (jj, hg), tinv, rhs in zip(chains, tinvs, rhss):
            sol = jnp.dot(expand(tinv), rhs, preferred_element_type=F32)
            u_ref[jj, hg] = sol[:, 0:DN_DV]
            w = sol[:, DN_DV:DN_DV + DN_DK].astype(BF16)
            for r in range(DN_GROUP):
                wq_ref[jj, hg * DN_GROUP + r, 0:DN_CHUNK, :] = w[r * DN_CHUNK:(r + 1) * DN_CHUNK]
        yield

    def recurrence(jjs):
        hslice = lambda h: slice(h * DN_DV, (h + 1) * DN_DV)
        for jj in jjs:
            rows = chunk_rows(jj)
            eg_row = aux_ref[0, pl.ds(rows.start, 1), :]
            states = [s_ref[:, hslice(h)] for h in range(DN_V_HEADS)]
            res = [jnp.dot(wq_ref[jj, h], states[h].astype(BF16), preferred_element_type=F32)
                   for h in range(DN_V_HEADS)]
            yield
            vnew = []
            for h in range(DN_V_HEADS):
                hg, r = divmod(h, DN_GROUP)
                vnew.append(u_ref[jj, hg, r * DN_CHUNK:(r + 1) * DN_CHUNK, :] - res[h][0:DN_CHUNK])
            for h in range(DN_V_HEADS):
                hg, r = divmod(h, DN_GROUP)
                eg = eg_row[:, AUX_EG * DN_V_HEADS + h:AUX_EG * DN_V_HEADS + h + 1]
                kd = kd_ref[jj, hg, r * DN_CHUNK:(r + 1) * DN_CHUNK, :]
                s_ref[:, hslice(h)] = states[h] * eg + _mm_tn(kd, vnew[h])
            yield
            for hg in range(ngroups):
                hs = range(hg * DN_GROUP, (hg + 1) * DN_GROUP)
                out = (jnp.concatenate([res[h][DN_CHUNK:2 * DN_CHUNK] for h in hs], axis=0)
                       + _mm(aqk_ref[jj, hg], jnp.concatenate([vnew[h] for h in hs], axis=0)))
                for r, h in enumerate(hs):
                    o_ref[0, rows, hslice(h)] = out[r * DN_CHUNK:(r + 1) * DN_CHUNK].astype(o_ref.dtype)
            yield

    stages = [range(i, min(i + DN_PIPE, cb)) for i in range(0, cb, DN_PIPE)]
    _interleave(local_work(stages[0]))
    for prev, cur in zip(stages, stages[1:]):
        _interleave(local_work(cur), recurrence(prev))
    _interleave(recurrence(stages[-1]))


def _delta_rule(q, k, v, aux, *, batch, seq, cb):
    t = q.shape[0]
    rb = cb * DN_CHUNK
    nb = seq // rb
    ngroups = DN_V_HEADS // DN_GROUP

    def rblock(b, d, j):
        return b * nb + jnp.where(d == 0, j, nb - 1 - j)

    return pl.pallas_call(
        functools.partial(_delta_rule_kernel, cb=cb),
        grid=(batch, 2, nb),
        in_specs=[pl.BlockSpec((rb, DN_QK_W), lambda b, d, j: (rblock(b, d, j), 0)),
                  pl.BlockSpec((rb, DN_QK_W), lambda b, d, j: (rblock(b, d, j), 0)),
                  pl.BlockSpec((rb, DN_V_W), lambda b, d, j: (rblock(b, d, j), 0)),
                  pl.BlockSpec((1, rb, AUX_W), lambda b, d, j: (d, rblock(b, d, j), 0))],
        out_specs=pl.BlockSpec((1, rb, DN_V_W), lambda b, d, j: (d, rblock(b, d, j), 0)),
        out_shape=jax.ShapeDtypeStruct((2, t, DN_V_W), BF16),
        scratch_shapes=[pltpu.VMEM((DN_DK, DN_V_HEADS * DN_DV), F32),
                        pltpu.VMEM((cb, DN_V_HEADS, 2 * DN_CHUNK, DN_DK), BF16),
                        pltpu.VMEM((cb, ngroups, DN_GM, DN_DV), F32),
                        pltpu.VMEM((cb, ngroups, DN_GM, DN_DK), BF16),
                        pltpu.VMEM((cb, ngroups, DN_GM, DN_GM), BF16)],
        compiler_params=_cparams(("parallel", "parallel", "arbitrary")),
        name="delta_rule",
    )(q, k, v, aux)


def _attention_kernel(qt_ref, k_ref, vt_ref, o_ref, acc_ref, m_ref, l_ref, *, tk, seq):
    m_ref[...] = jnp.full(m_ref.shape, -jnp.inf, F32)
    l_ref[...] = jnp.zeros(l_ref.shape, F32)
    acc_ref[...] = jnp.zeros(acc_ref.shape, F32)

    def step(c, carry):
        ks = pl.ds(pl.multiple_of(c * tk, tk), tk)
        kc = k_ref[ks, :]
        vtc = vt_ref[:, ks]
        for g in range(ATT_GROUP):
            s = jnp.dot(kc, qt_ref[g * HEAD_DIM:(g + 1) * HEAD_DIM, :], preferred_element_type=F32)
            m_old = m_ref[g:g + 1, :]
            m_new = jnp.maximum(m_old, jnp.max(s, axis=0, keepdims=True))
            alpha = jnp.exp2(m_old - m_new)
            p = jnp.exp2(s - m_new)
            l_ref[g:g + 1, :] = alpha * l_ref[g:g + 1, :] + jnp.sum(p, axis=0, keepdims=True)
            m_ref[g:g + 1, :] = m_new
            acc_ref[g] = alpha * acc_ref[g] + jnp.dot(vtc, p.astype(BF16), preferred_element_type=F32)
        return carry

    nsteps = seq // tk
    lax.fori_loop(0, nsteps, step, 0, unroll=math.gcd(nsteps, ATT_UNROLL))
    for g in range(ATT_GROUP):
        out = acc_ref[g] / l_ref[g:g + 1, :]
        o_ref[:, g * HEAD_DIM:(g + 1) * HEAD_DIM] = jnp.transpose(out).astype(o_ref.dtype)


def _attention(aqt, ak, avt, *, batch, seq, tq, tk):
    t = ak.shape[0]
    nq = seq // tq
    gw = ATT_GROUP * HEAD_DIM
    return pl.pallas_call(
        functools.partial(_attention_kernel, tk=tk, seq=seq),
        grid=(batch, ATT_KV_HEADS, nq),
        in_specs=[pl.BlockSpec((gw, tq), lambda b, g, i: (g, b * nq + i)),
                  pl.BlockSpec((seq, HEAD_DIM), lambda b, g, i: (b, g)),
                  pl.BlockSpec((HEAD_DIM, seq), lambda b, g, i: (g, b))],
        out_specs=pl.BlockSpec((tq, gw), lambda b, g, i: (b * nq + i, g)),
        out_shape=jax.ShapeDtypeStruct((t, ATT_Q_W), BF16),
        scratch_shapes=[pltpu.VMEM((ATT_GROUP, HEAD_DIM, tq), F32), pltpu.VMEM((SUBLANES, tq), F32),
                        pltpu.VMEM((SUBLANES, tq), F32)],
        compiler_params=_cparams(("parallel", "parallel", "parallel")),
        name="attention",
    )(aqt, ak, avt)


def _merge_mlp_kernel(h_ref, o_ref, z_ref, att_ref, gd_ref, ga_ref, pe_ref, on_ref, nm_ref, np_ref,
                      wdn_ref, wat_ref, wo_ref, wup_ref, wdown_ref, wpg_ref, wple_ref, out_ref):
    dot = lambda a, w_ref: jnp.dot(a, w_ref[...], preferred_element_type=F32)
    o = o_ref[0].astype(F32) + o_ref[1].astype(F32)
    z = z_ref[...].astype(F32)
    parts = []
    for h in range(DN_V_HEADS):
        sl = slice(h * DN_DV, (h + 1) * DN_DV)
        parts.append((_rms(o[:, sl], on_ref[...]) * _silu(z[:, sl])).astype(BF16))
    y_dn = dot(jnp.concatenate(parts, axis=1), wdn_ref)
    y_at = dot(att_ref[...], wat_ref)
    merged = _sigmoid(gd_ref[...].astype(F32)) * y_dn + _sigmoid(ga_ref[...].astype(F32)) * y_at
    h1 = h_ref[...] + dot(merged.astype(BF16), wo_ref)
    hm = dot(_rms(h1, nm_ref[...]).astype(BF16), wup_ref)
    h2 = h1 + dot(jnp.square(jnp.maximum(hm, 0.0)).astype(BF16), wdown_ref)
    gate = _sigmoid(dot(_rms(h2, np_ref[...]).astype(BF16), wpg_ref))
    out_ref[...] = h2 + gate * dot(pe_ref[...].astype(BF16), wple_ref)


def _merge_mlp(h, o, z, att, gd, ga, pe, on, nm, npl, wdn, wat, wo, wup, wdown, wpg, wple, *, tm):
    t = h.shape[0]
    row = lambda w: pl.BlockSpec((tm, w), lambda i: (i, 0))
    consts = (on, nm, npl, wdn, wat, wo, wup, wdown, wpg, wple)
    return pl.pallas_call(
        _merge_mlp_kernel,
        grid=(t // tm,),
        in_specs=[row(D_MODEL), pl.BlockSpec((2, tm, DN_V_W), lambda i: (0, i, 0)), row(DN_V_W),
                  row(ATT_Q_W), row(D_MODEL), row(D_MODEL), row(PLE_DIM)]
        + [_resident(c.shape) for c in consts],
        out_specs=row(D_MODEL),
        out_shape=jax.ShapeDtypeStruct((t, D_MODEL), F32),
        compiler_params=_cparams(("parallel",)),
        name="merge_mlp",
    )(h, o, z, att, gd, ga, pe, *consts)


def _rope_tables(seq):
    pos = np.arange(seq)
    r = (pos // GRID_W).astype(np.float64)
    c = (pos % GRID_W).astype(np.float64)
    half = HEAD_DIM // 2
    inv = np.power(np.float64(ROPE_THETA), -np.arange(0, half, 2, dtype=np.float64) / half)
    ar = r[:, None] * inv
    ac = c[:, None] * inv
    cos = np.concatenate([np.cos(ar), np.cos(ar), np.cos(ac), np.cos(ac)], axis=1)
    sin = np.concatenate([-np.sin(ar), np.sin(ar), -np.sin(ac), np.sin(ac)], axis=1)
    return jnp.asarray(cos, F32), jnp.asarray(sin, F32)


def _prepare_weights(norm_mix, w_in, dn_conv_w, dn_a_log, dn_dt_bias, dn_out_norm, w_dn_branch,
                     q_norm, k_norm, w_attn_branch, w_o, norm_mlp, w_up, w_down, norm_ple,
                     w_ple_gate, w_ple):
    bf = lambda w: w.astype(BF16)
    w_in = w_in[0]
    splits = (2 * DN_QK_W + DN_V_W, DN_V_W, 2 * DN_V_HEADS, 2 * DN_V_HEADS, ATT_Q_W, ATT_KV_W, ATT_KV_W,
              D_MODEL, D_MODEL)
    cols, start = [], 0
    for s in splits:
        cols.append(w_in[:, start:start + s])
        start += s
    w_qkv, w_z, w_b, w_a, w_aq, w_ak, w_av, w_gd, w_ga = cols
    pad = jnp.zeros((D_MODEL, LANES - 2 * DN_V_HEADS), F32)
    w_ba = jnp.concatenate([w_b, pad, w_a, pad], axis=1)
    in_ws = tuple(bf(w) for w in (w_qkv, w_z, w_ba, w_aq, w_ak, w_av, w_gd, w_ga))
    lane_pad = lambda x: jnp.pad(x.reshape(1, -1), ((0, 0), (0, LANES - x.size)))
    return dict(
        norm_mix=norm_mix[0].reshape(1, -1), in_ws=in_ws,
        conv=dn_conv_w[0],
        alog=lane_pad(dn_a_log[0]), dtb=lane_pad(dn_dt_bias[0]),
        out_norm=dn_out_norm[0].reshape(1, -1), w_dn=bf(w_dn_branch[0]),
        q_norm=q_norm[0].reshape(1, -1), k_norm=k_norm[0].reshape(1, -1),
        w_at=bf(w_attn_branch[0]), w_o=bf(w_o[0]),
        norm_mlp=norm_mlp[0].reshape(1, -1), w_up=bf(w_up[0]), w_down=bf(w_down[0]),
        norm_ple=norm_ple[0].reshape(1, -1), w_pg=bf(w_ple_gate[0]), w_ple=bf(w_ple[0]))


def _tiles(seq):
    return dict(tm_proj=min(256, seq), cb=min(16, seq // DN_CHUNK),
                tq=min(256, seq), tk=min(128, seq), tm_merge=min(256, seq))


def _encoder_layer(x, pe, p):
    batch, seq, _ = x.shape
    assert seq % DN_CHUNK == 0 and seq % GRID_W == 0
    t = batch * seq
    tl = _tiles(seq)
    h = x.reshape(t, D_MODEL)
    cos, sin = _rope_tables(seq)
    qn, kn, vc, aux, z, aq, ak, av, gd, ga = _in_proj(
        h, p["norm_mix"], cos, sin, p["q_norm"], p["k_norm"], p["in_ws"], p["conv"], p["alog"], p["dtb"],
        seq=seq, tm=tl["tm_proj"])
    o = _delta_rule(qn, kn, vc, aux, batch=batch, seq=seq, cb=tl["cb"])
    att = _attention(aq, ak, av, batch=batch, seq=seq, tq=tl["tq"], tk=tl["tk"])
    h3 = _merge_mlp(h, o, z, att, gd, ga, pe.reshape(t, PLE_DIM), p["out_norm"], p["norm_mlp"], p["norm_ple"],
                    p["w_dn"], p["w_at"], p["w_o"], p["w_up"], p["w_down"], p["w_pg"], p["w_ple"],
                    tm=tl["tm_merge"])
    return h3.reshape(batch, seq, D_MODEL)


@jax.jit
def kernel(x_prompt, x_sample, p_prompt, p_sample, norm_mix, w_in, dn_conv_w, dn_a_log, dn_dt_bias,
           dn_out_norm, w_dn_branch, q_norm, k_norm, w_attn_branch, w_o, norm_mlp, w_up, w_down,
           norm_ple, w_ple_gate, w_ple):
    assert norm_mix.shape[0] == 1, "single-layer trunk"
    p = _prepare_weights(norm_mix, w_in, dn_conv_w, dn_a_log, dn_dt_bias, dn_out_norm, w_dn_branch,
                         q_norm, k_norm, w_attn_branch, w_o, norm_mlp, w_up, w_down, norm_ple,
                         w_ple_gate, w_ple)
    y_prompt = _encoder_layer(x_prompt, p_prompt[0], p)
    y_sample = _encoder_layer(x_sample, p_sample[0], p)
    return (y_prompt, y_sample)
```
